```python
import jax, jax.numpy as jnp
from jax import lax
import numpy as np

D_MODEL = 1024
BATCH = 4
SEQ = 8192
DEPTH = 1

CHUNK = 64
MIX_WIDTH = D_MODEL
LRU_WIDTH = MIX_WIDTH // 2
LRU_HEADS = 8
LRU_HEAD_DIM = LRU_WIDTH // LRU_HEADS
CONV_WIDTH = 4
LRU_C = 8.0
ATTN_WIDTH = MIX_WIDTH - LRU_WIDTH
FOX_HEADS = 8
FOX_HEAD_DIM = ATTN_WIDTH // FOX_HEADS
Q_BLOCK = 128
IN_COLS = 2 * LRU_WIDTH + 3 * ATTN_WIDTH + FOX_HEADS
PEER_HEADS = 8
PEER_NKEYS = 128
PEER_N_EXPERTS = PEER_NKEYS * PEER_NKEYS
PEER_QDIM = 256
PEER_HALF = PEER_QDIM // 2
PEER_TOPK = 16
PEER_TOKEN_BLOCK = 128
RMS_EPS = 1e-6

kernel_name = 'hymba_rglru_fox_peer_block'


def rms_norm(x, g):
    xf = x.astype(jnp.float32)
    y = xf * lax.rsqrt(jnp.mean(xf * xf, axis=-1, keepdims=True) + RMS_EPS)
    return (y * g.astype(jnp.float32)).astype(x.dtype)


def causal_depthwise_conv(x, w, b):
    y = lax.conv_general_dilated(
        x, w[:, None, :].astype(x.dtype), window_strides=(1,),
        padding=[(CONV_WIDTH - 1, 0)],
        dimension_numbers=('NWC', 'WIO', 'NWC'),
        feature_group_count=x.shape[-1])
    return y + b.astype(x.dtype)


def rg_lru(x, w_a, b_a, w_x, b_x, lam):
    bsz, s, _ = x.shape
    xf = x.astype(jnp.float32)
    xh = xf.reshape(bsz, s, LRU_HEADS, LRU_HEAD_DIM)
    r = jax.nn.sigmoid(jnp.einsum('bshi,hij->bshj', xh, w_a.astype(jnp.float32)).reshape(bsz, s, LRU_WIDTH) + b_a.astype(jnp.float32))
    i = jax.nn.sigmoid(jnp.einsum('bshi,hij->bshj', xh, w_x.astype(jnp.float32)).reshape(bsz, s, LRU_WIDTH) + b_x.astype(jnp.float32))
    log_a = -LRU_C * r * jax.nn.softplus(-lam.astype(jnp.float32))
    a = jnp.exp(log_a)
    u = jnp.sqrt(-jnp.expm1(2.0 * log_a)) * (i * xf)

    def combine(left, right):
        a1, b1 = left
        a2, b2 = right
        return a1 * a2, a2 * b1 + b2

    _, h = lax.associative_scan(combine, (a, u), axis=1)
    return h


def forgetting_attention(q, k, v, log_f):
    bsz, s, nh, dh = q.shape
    F = jnp.cumsum(log_f.astype(jnp.float32), axis=1).transpose(0, 2, 1)
    qh = q.astype(jnp.float32).transpose(0, 2, 1, 3) * (dh ** -0.5)
    kh = k.astype(jnp.float32).transpose(0, 2, 1, 3)
    vh = v.astype(jnp.float32).transpose(0, 2, 1, 3)
    kpos = jnp.arange(s)

    def block(bi):
        start = bi * Q_BLOCK
        qb = lax.dynamic_slice_in_dim(qh, start, Q_BLOCK, axis=2)
        fq = lax.dynamic_slice_in_dim(F, start, Q_BLOCK, axis=2)
        qpos = start + jnp.arange(Q_BLOCK)
        logits = jnp.einsum('bhqd,bhkd->bhqk', qb, kh) + fq[..., :, None] - F[..., None, :]
        logits = jnp.where(kpos[None, :] <= qpos[:, None], logits, -jnp.inf)
        p = jax.nn.softmax(logits, axis=-1)
        return jnp.einsum('bhqk,bhkd->bhqd', p, vh)

    out = lax.map(block, jnp.arange(s // Q_BLOCK))
    return out.transpose(1, 0, 3, 2, 4).reshape(bsz, s, nh * dh)


def peer(h, w_q, subkeys, u_tab, v_tab):
    bsz, s, d = h.shape
    t = bsz * s
    hf = h.reshape(t, d)
    q = (hf @ w_q).astype(jnp.float32).reshape(t, PEER_HEADS, 2, PEER_HALF)
    sc = jnp.einsum('thpc,hpnc->thpn', q, subkeys.astype(jnp.float32))
    s1, i1 = lax.top_k(sc[:, :, 0], PEER_TOPK)
    s2, i2 = lax.top_k(sc[:, :, 1], PEER_TOPK)
    cand = (s1[..., :, None] + s2[..., None, :]).reshape(t, PEER_HEADS, PEER_TOPK * PEER_TOPK)
    top_s, top_p = lax.top_k(cand, PEER_TOPK)
    e1 = jnp.take_along_axis(i1, top_p // PEER_TOPK, axis=-1)
    e2 = jnp.take_along_axis(i2, top_p % PEER_TOPK, axis=-1)
    hk = PEER_HEADS * PEER_TOPK
    experts = (e1 * PEER_NKEYS + e2).reshape(t, hk)
    gates = jax.nn.softmax(top_s, axis=-1).reshape(t, hk)
    nb = t // PEER_TOKEN_BLOCK

    def block(args):
        xb, eb, gb = args
        pre = jnp.einsum('td,tkd->tk', xb.astype(jnp.float32), u_tab[eb].astype(jnp.float32))
        act = jax.nn.gelu(pre) * gb
        return jnp.einsum('tk,tkd->td', act, v_tab[eb].astype(jnp.float32))

    y = lax.map(block, (hf.reshape(nb, PEER_TOKEN_BLOCK, d),
                        experts.reshape(nb, PEER_TOKEN_BLOCK, hk),
                        gates.reshape(nb, PEER_TOKEN_BLOCK, hk)))
    return y.reshape(bsz, s, d).astype(h.dtype)


def setup_inputs(seed: int = 0) -> dict:
    key = jax.random.key(seed)
    ks = jax.random.split(key, 24)
    f32 = jnp.float32

    def nrm(k, shape, scale):
        return jax.random.normal(k, shape, f32) * scale

    a0 = jax.random.uniform(ks[9], (DEPTH, LRU_WIDTH), f32, 0.9, 0.999)
    return {
        'x': nrm(ks[0], (BATCH, SEQ, D_MODEL), 1.0),
        'norm1_g': 1.0 + nrm(ks[1], (DEPTH, D_MODEL), 0.01),
        'w_in': nrm(ks[2], (DEPTH, D_MODEL, IN_COLS), D_MODEL ** -0.5),
        'conv_w': nrm(ks[3], (DEPTH, CONV_WIDTH, LRU_WIDTH), CONV_WIDTH ** -0.5),
        'conv_b': nrm(ks[4], (DEPTH, LRU_WIDTH), 0.01),
        'lru_wa': nrm(ks[5], (DEPTH, LRU_HEADS, LRU_HEAD_DIM, LRU_HEAD_DIM), LRU_HEAD_DIM ** -0.5),
        'lru_ba': nrm(ks[6], (DEPTH, LRU_WIDTH), 0.01),
        'lru_wx': nrm(ks[7], (DEPTH, LRU_HEADS, LRU_HEAD_DIM, LRU_HEAD_DIM), LRU_HEAD_DIM ** -0.5),
        'lru_bx': nrm(ks[8], (DEPTH, LRU_WIDTH), 0.01),
        'lru_lambda': jnp.log(a0) - jnp.log1p(-a0),
        'fox_bf': jax.random.uniform(ks[10], (DEPTH, FOX_HEADS), f32, 1.0, 4.0),
        'gn_lru_g': 1.0 + nrm(ks[11], (DEPTH, LRU_WIDTH), 0.01),
        'gn_fox_g': 1.0 + nrm(ks[12], (DEPTH, ATTN_WIDTH), 0.01),
        'w_out': nrm(ks[13], (DEPTH, MIX_WIDTH, D_MODEL), MIX_WIDTH ** -0.5),
        'norm2_g': 1.0 + nrm(ks[14], (DEPTH, D_MODEL), 0.01),
        'peer_wq': nrm(ks[15], (DEPTH, D_MODEL, PEER_HEADS * PEER_QDIM), D_MODEL ** -0.5),
        'peer_subkeys': nrm(ks[16], (DEPTH, PEER_HEADS, 2, PEER_NKEYS, PEER_HALF), PEER_HALF ** -0.5),
        'peer_u': nrm(ks[17], (DEPTH, PEER_N_EXPERTS, D_MODEL), D_MODEL ** -0.5),
        'peer_v': nrm(ks[18], (DEPTH, PEER_N_EXPERTS, D_MODEL), 0.25),
        'final_g': 1.0 + nrm(ks[19], (D_MODEL,), 0.01),
    }


def reference(x, norm1_g, w_in, conv_w, conv_b, lru_wa, lru_ba, lru_wx, lru_bx,
              lru_lambda, fox_bf, gn_lru_g, gn_fox_g, w_out, norm2_g, peer_wq,
              peer_subkeys, peer_u, peer_v, final_g):
    bsz, s, _ = x.shape
    cuts = [LRU_WIDTH, 2 * LRU_WIDTH, 2 * LRU_WIDTH + ATTN_WIDTH,
            2 * LRU_WIDTH + 2 * ATTN_WIDTH, 2 * LRU_WIDTH + 3 * ATTN_WIDTH]
    for l in range(DEPTH):
        h = rms_norm(x, norm1_g[l])
        proj = h @ w_in[l]
        xr, gr, q, k, v, fl = jnp.split(proj, cuts, axis=-1)
        xr = causal_depthwise_conv(xr, conv_w[l], conv_b[l])
        y_lru = rg_lru(xr, lru_wa[l], lru_ba[l], lru_wx[l], lru_bx[l], lru_lambda[l])
        y_lru = y_lru * jax.nn.gelu(gr.astype(jnp.float32))
        log_f = jax.nn.log_sigmoid(fl.astype(jnp.float32) + fox_bf[l].astype(jnp.float32))
        y_fox = forgetting_attention(
            q.reshape(bsz, s, FOX_HEADS, FOX_HEAD_DIM),
            k.reshape(bsz, s, FOX_HEADS, FOX_HEAD_DIM),
            v.reshape(bsz, s, FOX_HEADS, FOX_HEAD_DIM), log_f)
        y_mix = jnp.concatenate([rms_norm(y_lru, gn_lru_g[l]),
                                 rms_norm(y_fox, gn_fox_g[l])], axis=-1).astype(x.dtype)
        x = x + y_mix @ w_out[l]
        h2 = rms_norm(x, norm2_g[l])
        x = x + peer(h2, peer_wq[l], peer_subkeys[l], peer_u[l], peer_v[l])
    return rms_norm(x, final_g)
```

```python
import functools

import jax
import jax.numpy as jnp
import numpy as np
from jax import lax
from jax.experimental import pallas as pl
from jax.experimental.pallas import tpu as pltpu

F32 = jnp.float32
BF16 = jnp.bfloat16

RMS_EPS = 1e-6
LRU_C = 8.0
CONV_WIDTH = 4
LRU_HEADS = 8
FOX_HEADS = 8
PEER_HEADS = 8
PEER_NKEYS = 128
PEER_TOPK = 16

LANES = 128
SUBLANES = 8
VMEM_LIMIT = 56 * 1024 * 1024

NEG_INF = float("-inf")


def _rms(x, g):
    return x * lax.rsqrt(jnp.mean(x * x, axis=-1, keepdims=True) + RMS_EPS) * g


def _inproj_body(x_ref, g_ref, w_ref, wfl_ref, bf_ref,
                 xr_ref, gr_ref, q_ref, k_ref, v_ref, ft_ref, carry_ref,
                 *, tm, width, q_scale):
    s = pl.program_id(1)

    @pl.when(s == 0)
    def _():
        carry_ref[...] = jnp.zeros_like(carry_ref)

    hb = _rms(x_ref[0], g_ref[...]).astype(BF16)

    def piece(i):
        return jnp.dot(hb, w_ref[:, i * width:(i + 1) * width],
                       preferred_element_type=F32)

    xr_ref[0] = piece(0)
    gr_ref[0] = piece(1)
    q_ref[0] = (piece(2) * q_scale).astype(BF16)
    k_ref[0] = piece(3).astype(BF16)
    v_ref[0] = piece(4).astype(BF16)

    fl = jnp.dot(hb, wfl_ref[...], preferred_element_type=F32)
    c = jax.nn.log_sigmoid(fl + bf_ref[...])
    row = lax.broadcasted_iota(jnp.int32, c.shape, 0)
    sh = 1
    while sh < tm:
        c = c + jnp.where(row >= sh, pltpu.roll(c, sh, axis=0), 0.0)
        sh *= 2
    f = c + carry_ref[...]
    carry_ref[...] = f[tm - 1:tm, :]
    ft_ref[0] = f.T[0:FOX_HEADS, :]


def _inproj(x, g, w_main, w_fl, b_fl, *, tm):
    bsz, seq, d = x.shape
    width = w_main.shape[1] // 5
    body = functools.partial(_inproj_body, tm=tm, width=width,
                             q_scale=(width // FOX_HEADS) ** -0.5)
    tok = lambda b, s: (b, s, 0)
    const = lambda b, s: (0, 0)
    big = jax.ShapeDtypeStruct((bsz, seq, width), F32)
    bigb = jax.ShapeDtypeStruct((bsz, seq, width), BF16)
    return pl.pallas_call(
        body,
        grid=(bsz, seq // tm),
        in_specs=[
            pl.BlockSpec((1, tm, d), tok),
            pl.BlockSpec((1, d), const),
            pl.BlockSpec(w_main.shape, const),
            pl.BlockSpec(w_fl.shape, const),
            pl.BlockSpec((1, LANES), const),
        ],
        out_specs=[pl.BlockSpec((1, tm, width), tok)] * 5
        + [pl.BlockSpec((1, FOX_HEADS, tm), lambda b, s: (b, 0, s))],
        out_shape=[big, big, bigb, bigb, bigb,
                   jax.ShapeDtypeStruct((bsz, FOX_HEADS, seq), F32)],
        scratch_shapes=[pltpu.VMEM((1, LANES), F32)],
        compiler_params=pltpu.CompilerParams(
            dimension_semantics=("parallel", "arbitrary"),
            vmem_limit_bytes=VMEM_LIMIT),
        name="inproj",
    )(x, g, w_main, w_fl, b_fl)


def _rglru_body(xr_ref, gr_ref, cw_ref, cb_ref, wg_ref, bg_ref, lam_ref, gn_ref,
                y_ref, xbuf_ref, hc_ref, *, ts, width):
    s = pl.program_id(1)

    @pl.when(s == 0)
    def _():
        xbuf_ref[0:SUBLANES, :] = jnp.zeros((SUBLANES, width), F32)
        hc_ref[...] = jnp.zeros_like(hc_ref)

    x = xr_ref[0]
    xbuf_ref[SUBLANES:SUBLANES + ts, :] = x
    cw = cw_ref[...]
    conv = cb_ref[...] + cw[CONV_WIDTH - 1:CONV_WIDTH, :] * x
    for back in range(1, CONV_WIDTH):
        k = CONV_WIDTH - 1 - back
        conv = conv + cw[k:k + 1, :] * xbuf_ref[SUBLANES - back:SUBLANES - back + ts, :]
    xbuf_ref[0:SUBLANES, :] = x[ts - SUBLANES:ts, :]

    gates = jnp.dot(conv.astype(BF16), wg_ref[...],
                    preferred_element_type=F32) + bg_ref[...]
    r = jax.nn.sigmoid(gates[:, :width])
    i = jax.nn.sigmoid(gates[:, width:])
    log_a = (-LRU_C) * r * jax.nn.softplus(-lam_ref[...])
    a = jnp.exp(log_a)
    u = jnp.sqrt(1.0 - a * a) * (i * conv)

    row = lax.broadcasted_iota(jnp.int32, a.shape, 0)
    sh = 1
    while sh < ts:
        keep = row >= sh
        a_s = jnp.where(keep, pltpu.roll(a, sh, axis=0), 1.0)
        u_s = jnp.where(keep, pltpu.roll(u, sh, axis=0), 0.0)
        u = a * u_s + u
        a = a * a_s
        sh *= 2
    h = u + a * hc_ref[...]
    hc_ref[...] = h[ts - 1:ts, :]

    y = h * jax.nn.gelu(gr_ref[0])
    y_ref[0] = _rms(y, gn_ref[...]).astype(BF16)


def _rglru(xr, gr, conv_w, conv_b, w_gate, b_gate, lam, gn, *, ts):
    bsz, seq, width = xr.shape
    body = functools.partial(_rglru_body, ts=ts, width=width)
    tok = lambda b, s: (b, s, 0)
    const = lambda b, s: (0, 0)
    return pl.pallas_call(
        body,
        grid=(bsz, seq // ts),
        in_specs=[
            pl.BlockSpec((1, ts, width), tok),
            pl.BlockSpec((1, ts, width), tok),
            pl.BlockSpec(conv_w.shape, const),
            pl.BlockSpec((1, width), const),
            pl.BlockSpec(w_gate.shape, const),
            pl.BlockSpec((1, 2 * width), const),
            pl.BlockSpec((1, width), const),
            pl.BlockSpec((1, width), const),
        ],
        out_specs=pl.BlockSpec((1, ts, width), tok),
        out_shape=jax.ShapeDtypeStruct((bsz, seq, width), BF16),
        scratch_shapes=[pltpu.VMEM((ts + SUBLANES, width), F32),
                        pltpu.VMEM((1, width), F32)],
        compiler_params=pltpu.CompilerParams(
            dimension_semantics=("parallel", "arbitrary"),
            vmem_limit_bytes=VMEM_LIMIT),
        name="rglru",
    )(xr, gr, conv_w, conv_b, w_gate, b_gate, lam, gn)


def _fox_body(qi_ref, kj_ref, q_ref, k_ref, v_ref, fq_ref, fk_ref, o_ref,
              m_ref, l_ref, acc_ref, *, tq, tk, dh):
    t = pl.program_id(2)
    qi = qi_ref[t]
    kj = kj_ref[t]

    @pl.when(kj == 0)
    def _():
        m_ref[...] = jnp.full_like(m_ref, NEG_INF)
        l_ref[...] = jnp.zeros_like(l_ref)
        acc_ref[...] = jnp.zeros_like(acc_ref)

    lane = lax.broadcasted_iota(jnp.int32, (1, 2 * dh), 1)

    def step(diagonal):
        q2 = q_ref[0]
        k2 = k_ref[0]
        v2 = v_ref[0]
        fq = fq_ref[0, 0]
        fk = fk_ref[0, 0]
        for hh in range(2):
            in_head = (lane >= hh * dh) & (lane < (hh + 1) * dh)
            qm = jnp.where(in_head, q2, jnp.zeros_like(q2))
            s = lax.dot_general(qm, k2, (((1,), (1,)), ((), ())),
                                preferred_element_type=F32)
            s = s + (fq[:, hh:hh + 1] - fk[hh:hh + 1, :])
            if diagonal:
                rr = lax.broadcasted_iota(jnp.int32, (tq, tk), 0)
                cc = lax.broadcasted_iota(jnp.int32, (tq, tk), 1)
                s = jnp.where(cc <= rr, s, NEG_INF)
            m_prev = m_ref[hh]
            m_new = jnp.maximum(m_prev, jnp.max(s, axis=1, keepdims=True))
            alpha = jnp.exp(m_prev - m_new)
            p = jnp.exp(s - m_new)
            l_ref[hh] = alpha * l_ref[hh] + jnp.sum(p, axis=1, keepdims=True)
            acc_ref[hh] = alpha * acc_ref[hh] + jnp.dot(
                p.astype(BF16), v2, preferred_element_type=F32)
            m_ref[hh] = m_new

    @pl.when(kj < qi)
    def _():
        step(False)

    @pl.when(kj == qi)
    def _():
        step(True)
        o0 = acc_ref[0] / l_ref[0]
        o1 = acc_ref[1] / l_ref[1]
        o_ref[0] = jnp.where(lane < dh, o0, o1)


def _fox(q, k, v, f_col, f_row, *, tq):
    bsz, seq, width = q.shape
    dh = width // FOX_HEADS
    pairs = FOX_HEADS // 2
    nq = seq // tq
    qi_tab = np.concatenate([np.full(i + 1, i, np.int32) for i in range(nq)])
    kj_tab = np.concatenate([np.arange(i + 1, dtype=np.int32) for i in range(nq)])
    body = functools.partial(_fox_body, tq=tq, tk=tq, dh=dh)
    grid_spec = pltpu.PrefetchScalarGridSpec(
        num_scalar_prefetch=2,
        grid=(bsz, pairs, len(qi_tab)),
        in_specs=[
            pl.BlockSpec((1, tq, 2 * dh), lambda b, p, t, qi, kj: (b, qi[t], p)),
            pl.BlockSpec((1, tq, 2 * dh), lambda b, p, t, qi, kj: (b, kj[t], p)),
            pl.BlockSpec((1, tq, 2 * dh), lambda b, p, t, qi, kj: (b, kj[t], p)),
            pl.BlockSpec((1, 1, tq, 2), lambda b, p, t, qi, kj: (b, p, qi[t], 0)),
            pl.BlockSpec((1, 1, 2, tq), lambda b, p, t, qi, kj: (b, p, 0, kj[t])),
        ],
        out_specs=pl.BlockSpec((1, tq, 2 * dh), lambda b, p, t, qi, kj: (b, qi[t], p)),
        scratch_shapes=[pltpu.VMEM((2, tq, 1), F32),
                        pltpu.VMEM((2, tq, 1), F32),
                        pltpu.VMEM((2, tq, 2 * dh), F32)],
    )
    return pl.pallas_call(
        body,
        grid_spec=grid_spec,
        out_shape=jax.ShapeDtypeStruct((bsz, seq, width), F32),
        compiler_params=pltpu.CompilerParams(
            dimension_semantics=("parallel", "parallel", "arbitrary"),
            vmem_limit_bytes=VMEM_LIMIT),
        name="fox",
    )(jnp.asarray(qi_tab), jnp.asarray(kj_tab), q, k, v, f_col, f_row)


def _mid_body(x_ref, yl_ref, yf_ref, gnf_ref, wo_ref, g2_ref, wqt_ref, sk_ref,
              x1_ref, h2t_ref, s1_ref, s2_ref, *, half):
    yfn = _rms(yf_ref[...], gnf_ref[...]).astype(BF16)
    mix = jnp.dot(yl_ref[...], wo_ref[0:half, :], preferred_element_type=F32)
    mix = mix + jnp.dot(yfn, wo_ref[half:2 * half, :], preferred_element_type=F32)
    x1 = x_ref[...] + mix
    x1_ref[...] = x1
    h2t = _rms(x1, g2_ref[...]).T.astype(BF16)
    h2t_ref[...] = h2t
    qpt = jnp.dot(wqt_ref[...], h2t, preferred_element_type=F32)
    for hp in range(2 * PEER_HEADS):
        qs = qpt[hp * LANES:(hp + 1) * LANES, :].astype(BF16)
        sc = jnp.dot(sk_ref[hp], qs, preferred_element_type=F32)
        if hp % 2 == 0:
            s1_ref[hp // 2] = sc
        else:
            s2_ref[hp // 2] = sc


def _mid(x2d, y_lru, y_fox, gn_fox, w_out, g2, wq_t, subkeys, *, tm):
    t, d = x2d.shape
    half = y_lru.shape[1]
    body = functools.partial(_mid_body, half=half)
    tok = lambda i: (i, 0)
    const2 = lambda i: (0, 0)
    sc_shape = jax.ShapeDtypeStruct((PEER_HEADS, PEER_NKEYS, t), F32)
    sc_spec = pl.BlockSpec((PEER_HEADS, PEER_NKEYS, tm), lambda i: (0, 0, i))
    return pl.pallas_call(
        body,
        grid=(t // tm,),
        in_specs=[
            pl.BlockSpec((tm, d), tok),
            pl.BlockSpec((tm, half), tok),
            pl.BlockSpec((tm, half), tok),
            pl.BlockSpec((1, half), const2),
            pl.BlockSpec(w_out.shape, const2),
            pl.BlockSpec((1, d), const2),
            pl.BlockSpec(wq_t.shape, const2),
            pl.BlockSpec(subkeys.shape, lambda i: (0, 0, 0)),
        ],
        out_specs=[pl.BlockSpec((tm, d), tok),
                   pl.BlockSpec((d, tm), lambda i: (0, i)),
                   sc_spec, sc_spec],
        out_shape=[jax.ShapeDtypeStruct((t, d), F32),
                   jax.ShapeDtypeStruct((d, t), BF16),
                   sc_shape, sc_shape],
        compiler_params=pltpu.CompilerParams(
            dimension_semantics=("parallel",),
            vmem_limit_bytes=VMEM_LIMIT),
        name="mid",
    )(x2d, y_lru, y_fox, gn_fox, w_out, g2, wq_t, subkeys)


_N_RANK = PEER_TOPK + 1
_CAND = [(a, b) for a in range(_N_RANK) for b in range(_N_RANK)
         if (a + 1) * (b + 1) <= _N_RANK]
_CAND_ROWS = -(-len(_CAND) // SUBLANES) * SUBLANES


def _top_values(x, n):
    vals = []
    for _ in range(n):
        mx = jnp.max(x, axis=0, keepdims=True)
        vals.append(mx)
        x = jnp.where(x == mx, NEG_INF, x)
    return vals


def _gates_body(s1_ref, s2_ref, th_ref, p1_ref, p2_ref, cand_ref):
    s1 = s1_ref[0]
    s2 = s2_ref[0]
    a = _top_values(s1, _N_RANK)
    b = _top_values(s2, _N_RANK)
    cand_ref[...] = jnp.full_like(cand_ref, NEG_INF)
    for r, (ia, ib) in enumerate(_CAND):
        cand_ref[r:r + 1, :] = a[ia] + b[ib]
    c = _top_values(cand_ref[...], _N_RANK)
    z = jnp.zeros_like(c[0])
    for r in range(PEER_TOPK):
        z = z + jnp.exp(c[r] - c[0])
    tau = 0.5 * (c[PEER_TOPK - 1] + c[PEER_TOPK])
    th_ref[0] = tau - s1
    p1_ref[0] = jnp.exp(s1 - a[0]) / z
    p2_ref[0] = jnp.exp(s2 - b[0])


def _gates(s1, s2, *, tm):
    heads, nk, t = s1.shape
    spec = pl.BlockSpec((1, nk, tm), lambda i, h: (h, 0, i))
    shape = jax.ShapeDtypeStruct(s1.shape, F32)
    return pl.pallas_call(
        _gates_body,
        grid=(t // tm, heads),
        in_specs=[spec, spec],
        out_specs=[spec, spec, spec],
        out_shape=[shape, shape, shape],
        scratch_shapes=[pltpu.VMEM((_CAND_ROWS, tm), F32)],
        compiler_params=pltpu.CompilerParams(
            dimension_semantics=("parallel", "parallel"),
            vmem_limit_bytes=VMEM_LIMIT),
        name="gates",
    )(s1, s2)


def _peer_body(h2t_ref, u_ref, vt_ref, s2_ref, p2_ref, th_ref, p1_ref, x1_ref, gf_ref,
               o_ref, acc_ref, pre_ref, wact_ref, *, rows_per_step):
    e = pl.program_id(1)

    @pl.when(e == 0)
    def _():
        acc_ref[...] = jnp.zeros_like(acc_ref)

    pre_ref[...] = jnp.dot(u_ref[...], h2t_ref[...], preferred_element_type=F32)

    def one_row(il, carry):
        i = e * rows_per_step + il
        w = jnp.zeros((PEER_NKEYS, pre_ref.shape[1]), F32)
        for h in range(PEER_HEADS):
            th = th_ref[h, pl.ds(i, 1), :]
            p1 = p1_ref[h, pl.ds(i, 1), :]
            w = w + jnp.where(s2_ref[h] >= th, p2_ref[h], 0.0) * p1
        off = pl.multiple_of(il * PEER_NKEYS, PEER_NKEYS)
        pre = pre_ref[pl.ds(off, PEER_NKEYS), :]
        wact_ref[pl.ds(off, PEER_NKEYS), :] = (jax.nn.gelu(pre) * w).astype(BF16)
        return carry

    lax.fori_loop(0, rows_per_step, one_row, 0)
    acc_ref[...] += jnp.dot(vt_ref[...], wact_ref[...], preferred_element_type=F32)

    @pl.when(e == pl.num_programs(1) - 1)
    def _():
        xo = x1_ref[...] + acc_ref[...].T
        o_ref[...] = _rms(xo, gf_ref[...])


def _peer(h2t, u, vt, s2, p2, th, p1, x1, gf, *, tm, te):
    d, t = h2t.shape
    n_exp = u.shape[0]
    rows_per_step = te // PEER_NKEYS
    body = functools.partial(_peer_body, rows_per_step=rows_per_step)
    hspec = pl.BlockSpec((PEER_HEADS, PEER_NKEYS, tm), lambda i, e: (0, 0, i))
    return pl.pallas_call(
        body,
        grid=(t // tm, n_exp // te),
        in_specs=[
            pl.BlockSpec((d, tm), lambda i, e: (0, i)),
            pl.BlockSpec((te, d), lambda i, e: (e, 0)),
            pl.BlockSpec((d, te), lambda i, e: (0, e)),
            hspec, hspec, hspec, hspec,
            pl.BlockSpec((tm, d), lambda i, e: (i, 0)),
            pl.BlockSpec((1, d), lambda i, e: (0, 0)),
        ],
        out_specs=pl.BlockSpec((tm, d), lambda i, e: (i, 0)),
        out_shape=jax.ShapeDtypeStruct((t, d), F32),
        scratch_shapes=[pltpu.VMEM((d, tm), F32),
                        pltpu.VMEM((te, tm), F32),
                        pltpu.VMEM((te, tm), BF16)],
        compiler_params=pltpu.CompilerParams(
            dimension_semantics=("parallel", "arbitrary"),
            vmem_limit_bytes=VMEM_LIMIT),
        name="peer",
    )(h2t, u, vt, s2, p2, th, p1, x1, gf)


def _block_diag(w):
    heads, dh, _ = w.shape
    eye = jnp.eye(heads, dtype=w.dtype)
    return jnp.einsum("hij,hg->higj", w, eye).reshape(heads * dh, heads * dh)


def _layer(x, norm1_g, w_in, conv_w, conv_b, lru_wa, lru_ba, lru_wx, lru_bx, lru_lambda,
           fox_bf, gn_lru_g, gn_fox_g, w_out, norm2_g, peer_wq, peer_subkeys, peer_u,
           peer_v, out_g):
    bsz, seq, d = x.shape
    width = lru_lambda.shape[0]
    row = lambda v: v.reshape(1, -1).astype(F32)

    w_main = w_in[:, :5 * width].astype(BF16)
    w_fl = jnp.pad(w_in[:, 5 * width:], ((0, 0), (0, LANES - FOX_HEADS))).astype(BF16)
    b_fl = jnp.pad(fox_bf.astype(F32), (0, LANES - FOX_HEADS)).reshape(1, LANES)
    xr, gr, q, k, v, f_t = _inproj(x, row(norm1_g), w_main, w_fl, b_fl, tm=512)

    w_gate = jnp.concatenate([_block_diag(lru_wa), _block_diag(lru_wx)], axis=1).astype(BF16)
    b_gate = jnp.concatenate([lru_ba, lru_bx]).reshape(1, -1).astype(F32)
    y_lru = _rglru(xr, gr, conv_w.astype(F32), row(conv_b), w_gate, b_gate,
                   row(lru_lambda), row(gn_lru_g), ts=256)

    f_row = f_t.reshape(bsz, FOX_HEADS // 2, 2, seq)
    f_col = f_row.transpose(0, 1, 3, 2)
    y_fox = _fox(q, k, v, f_col, f_row, tq=512)

    t = bsz * seq
    n_hp = 2 * PEER_HEADS
    subkeys = peer_subkeys.reshape(n_hp, PEER_NKEYS, -1).astype(BF16)
    x1, h2t, s1, s2 = _mid(x.reshape(t, d), y_lru.reshape(t, width), y_fox.reshape(t, width),
                           row(gn_fox_g), w_out.astype(BF16), row(norm2_g),
                           peer_wq.T.astype(BF16), subkeys, tm=256)

    th, p1, p2 = _gates(s1, s2, tm=256)
    out = _peer(h2t, peer_u.astype(BF16), peer_v.T.astype(BF16), s2, p2, th, p1, x1,
                row(out_g), tm=512, te=1024)
    return out.reshape(bsz, seq, d)


def kernel(x, norm1_g, w_in, conv_w, conv_b, lru_wa, lru_ba, lru_wx, lru_bx, lru_lambda,
           fox_bf, gn_lru_g, gn_fox_g, w_out, norm2_g, peer_wq, peer_subkeys, peer_u,
           peer_v, final_g):
    depth = w_in.shape[0]
    assert depth == 1, "the fused final RMSNorm assumes a single trunk layer"
    return _layer(x, norm1_g[0], w_in[0], conv_w[0], conv_b[0], lru_wa[0], lru_ba[0],
                  lru_wx[0], lru_bx[0], lru_lambda[0], fox_bf[0], gn_lru_g[0], gn_fox_g[0],
                  w_out[0], norm2_g[0], peer_wq[0], peer_subkeys[0], peer_u[0], peer_v[0],
                  final_g)
```

```python
import functools

import jax
import jax.numpy as jnp
import numpy as np
from jax import lax
from jax.experimental import pallas as pl
from jax.experimental.pallas import tpu as pltpu

F32 = jnp.float32
BF16 = jnp.bfloat16

RMS_EPS = 1e-6
LRU_C = 8.0
CONV_WIDTH = 4
LRU_HEADS = 8
FOX_HEADS = 8
PEER_HEADS = 8
PEER_NKEYS = 128
PEER_TOPK = 16
FOX_BLOCK = 512

LANES = 128
SUBLANES = 8
VMEM_LIMIT = 56 * 1024 * 1024

NEG_INF = float("-inf")
LOG2E = 1.4426950408889634


def _rms(x, g):
    return x * lax.rsqrt(jnp.mean(x * x, axis=-1, keepdims=True) + RMS_EPS) * g


def _inproj_body(x_ref, g_ref, w_ref, wfl_ref, bf_ref,
                 xr_ref, gr_ref, qt_ref, k_ref, vt_ref, ft_ref, carry_ref,
                 *, tm, width, q_scale):
    s = pl.program_id(1)

    @pl.when(s == 0)
    def _():
        carry_ref[...] = jnp.zeros_like(carry_ref)

    hb = _rms(x_ref[0], g_ref[...]).astype(BF16)

    def piece(i):
        return jnp.dot(hb, w_ref[:, i * width:(i + 1) * width],
                       preferred_element_type=F32)

    xr_ref[0] = piece(0)
    gr_ref[0] = piece(1)
    qt_ref[0, 0] = (piece(2) * (q_scale * LOG2E)).T.astype(BF16)
    k_ref[0] = piece(3).astype(BF16)
    vt_ref[0, 0] = piece(4).T.astype(BF16)

    fl = jnp.dot(hb, wfl_ref[...], preferred_element_type=F32)
    c = jax.nn.log_sigmoid(fl + bf_ref[...]) * LOG2E
    row = lax.broadcasted_iota(jnp.int32, c.shape, 0)
    sh = 1
    while sh < tm:
        c = c + jnp.where(row >= sh, pltpu.roll(c, sh, axis=0), 0.0)
        sh *= 2
    f = c + carry_ref[...]
    carry_ref[...] = f[tm - 1:tm, :]
    ft_ref[0] = f.T[0:FOX_HEADS, :]


def _inproj(x, g, w_main, w_fl, b_fl, *, tm):
    bsz, seq, d = x.shape
    width = w_main.shape[1] // 5
    body = functools.partial(_inproj_body, tm=tm, width=width,
                             q_scale=(width // FOX_HEADS) ** -0.5)
    tok = lambda b, s: (b, s, 0)
    const = lambda b, s: (0, 0)
    big = jax.ShapeDtypeStruct((bsz, seq, width), F32)
    bigb = jax.ShapeDtypeStruct((bsz, seq, width), BF16)
    blocked_t = jax.ShapeDtypeStruct((bsz, seq // tm, width, tm), BF16)
    return pl.pallas_call(
        body,
        grid=(bsz, seq // tm),
        in_specs=[
            pl.BlockSpec((1, tm, d), tok),
            pl.BlockSpec((1, d), const),
            pl.BlockSpec(w_main.shape, const),
            pl.BlockSpec(w_fl.shape, const),
            pl.BlockSpec((1, LANES), const),
        ],
        out_specs=[pl.BlockSpec((1, tm, width), tok),
                   pl.BlockSpec((1, tm, width), tok),
                   pl.BlockSpec((1, 1, width, tm), lambda b, s: (b, s, 0, 0)),
                   pl.BlockSpec((1, tm, width), tok),
                   pl.BlockSpec((1, 1, width, tm), lambda b, s: (b, s, 0, 0)),
                   pl.BlockSpec((1, FOX_HEADS, tm), lambda b, s: (b, 0, s))],
        out_shape=[big, big, blocked_t, bigb, blocked_t,
                   jax.ShapeDtypeStruct((bsz, FOX_HEADS, seq), F32)],
        scratch_shapes=[pltpu.VMEM((1, LANES), F32)],
        compiler_params=pltpu.CompilerParams(
            dimension_semantics=("parallel", "arbitrary"),
            vmem_limit_bytes=VMEM_LIMIT),
        name="inproj",
    )(x, g, w_main, w_fl, b_fl)


def _rglru_body(xr_ref, gr_ref, cw_ref, cb_ref, wg_ref, bg_ref, lam_ref, gn_ref,
                y_ref, xbuf_ref, hc_ref, *, ts, width):
    s = pl.program_id(1)

    @pl.when(s == 0)
    def _():
        xbuf_ref[0:SUBLANES, :] = jnp.zeros((SUBLANES, width), F32)
        hc_ref[...] = jnp.zeros_like(hc_ref)

    x = xr_ref[0]
    xbuf_ref[SUBLANES:SUBLANES + ts, :] = x
    cw = cw_ref[...]
    conv = cb_ref[...] + cw[CONV_WIDTH - 1:CONV_WIDTH, :] * x
    for back in range(1, CONV_WIDTH):
        k = CONV_WIDTH - 1 - back
        conv = conv + cw[k:k + 1, :] * xbuf_ref[SUBLANES - back:SUBLANES - back + ts, :]
    xbuf_ref[0:SUBLANES, :] = x[ts - SUBLANES:ts, :]

    gates = jnp.dot(conv.astype(BF16), wg_ref[...],
                    preferred_element_type=F32) + bg_ref[...]
    r = jax.nn.sigmoid(gates[:, :width])
    i = jax.nn.sigmoid(gates[:, width:])
    log_a = (-LRU_C) * r * jax.nn.softplus(-lam_ref[...])
    a = jnp.exp(log_a)
    u = jnp.sqrt(1.0 - a * a) * (i * conv)

    row = lax.broadcasted_iota(jnp.int32, a.shape, 0)
    sh = 1
    while sh < ts:
        keep = row >= sh
        a_s = jnp.where(keep, pltpu.roll(a, sh, axis=0), 1.0)
        u_s = jnp.where(keep, pltpu.roll(u, sh, axis=0), 0.0)
        u = a * u_s + u
        a = a * a_s
        sh *= 2
    h = u + a * hc_ref[...]
    hc_ref[...] = h[ts - 1:ts, :]

    y = h * jax.nn.gelu(gr_ref[0])
    y_ref[0] = _rms(y, gn_ref[...]).astype(BF16)


def _rglru(xr, gr, conv_w, conv_b, w_gate, b_gate, lam, gn, *, ts):
    bsz, seq, width = xr.shape
    body = functools.partial(_rglru_body, ts=ts, width=width)
    tok = lambda b, s: (b, s, 0)
    const = lambda b, s: (0, 0)
    return pl.pallas_call(
        body,
        grid=(bsz, seq // ts),
        in_specs=[
            pl.BlockSpec((1, ts, width), tok),
            pl.BlockSpec((1, ts, width), tok),
            pl.BlockSpec(conv_w.shape, const),
            pl.BlockSpec((1, width), const),
            pl.BlockSpec(w_gate.shape, const),
            pl.BlockSpec((1, 2 * width), const),
            pl.BlockSpec((1, width), const),
            pl.BlockSpec((1, width), const),
        ],
        out_specs=pl.BlockSpec((1, ts, width), tok),
        out_shape=jax.ShapeDtypeStruct((bsz, seq, width), BF16),
        scratch_shapes=[pltpu.VMEM((ts + SUBLANES, width), F32),
                        pltpu.VMEM((1, width), F32)],
        compiler_params=pltpu.CompilerParams(
            dimension_semantics=("parallel", "arbitrary"),
            vmem_limit_bytes=VMEM_LIMIT),
        name="rglru",
    )(xr, gr, conv_w, conv_b, w_gate, b_gate, lam, gn)


_ST_M, _ST_L, _ST_ALPHA, _ST_LFIN = 0, 2, 4, 8


def _fox_body(qi_ref, kj_ref, qt_ref, k_ref, vt_ref, fq_ref, fk_ref, o_ref,
              s0_ref, s1_ref, p0_ref, p1_ref, st_ref, acc_ref, mb_ref,
              *, n_items, tq, tk, dh):
    row_t = lax.broadcasted_iota(jnp.int32, (tk, tq), 0)
    col_t = lax.broadcasted_iota(jnp.int32, (tk, tq), 1)
    mb_ref[0] = jnp.zeros((tk, tq), F32)
    mb_ref[1] = jnp.where(row_t <= col_t, 0.0, NEG_INF)
    st_ref[...] = jnp.ones_like(st_ref)
    acc_ref[...] = jnp.zeros_like(acc_ref)
    p1_ref[...] = jnp.zeros_like(p1_ref)
    head_row = lax.broadcasted_iota(jnp.int32, (2 * dh, 1), 0)

    def stage_a(j, s_ref):
        qt = qt_ref[0, qi_ref[j]]
        kb = k_ref[0, pl.ds(pl.multiple_of(kj_ref[j] * tk, tk), tk), :]
        for h in range(2):
            in_head = (head_row >= h * dh) & (head_row < (h + 1) * dh)
            qm = jnp.where(in_head, qt, jnp.zeros_like(qt))
            s_ref[h] = jnp.dot(kb, qm, preferred_element_type=F32)

    def stage_b(j, s_ref, p_ref, buf):
        qi = qi_ref[j]
        kj = kj_ref[j]
        mb = mb_ref[(kj == qi).astype(jnp.int32)]
        fq2 = fq_ref[0, 0, qi]
        for h in range(2):
            fk = fk_ref[0, 0, pl.ds(pl.multiple_of(kj * tk, tk), tk), h:h + 1]
            fq = fq2[h:h + 1, :]
            z = s_ref[h] - fk + mb
            mz = jnp.max(z, axis=0, keepdims=True) + fq
            m_prev = jnp.where(kj == 0, NEG_INF, st_ref[_ST_M + h:_ST_M + h + 1, :])
            m_new = jnp.maximum(m_prev, mz)
            alpha = jnp.exp2(m_prev - m_new)
            p = jnp.exp2(z - (m_new - fq))
            l_new = (alpha * st_ref[_ST_L + h:_ST_L + h + 1, :]
                     + jnp.sum(p, axis=0, keepdims=True))
            p_ref[h] = p.astype(BF16)
            st_ref[_ST_M + h:_ST_M + h + 1, :] = m_new
            st_ref[_ST_L + h:_ST_L + h + 1, :] = l_new
            ra = _ST_ALPHA + 2 * buf + h
            st_ref[ra:ra + 1, :] = alpha
            rl = _ST_LFIN + 2 * buf + h
            st_ref[rl:rl + 1, :] = l_new

    def stage_c(j, p_ref, buf):
        qi = qi_ref[j]
        vt = vt_ref[0, kj_ref[j]]
        for h in range(2):
            rows = slice(h * dh, (h + 1) * dh)
            pv = jnp.dot(vt[rows, :], p_ref[h], preferred_element_type=F32)
            ra = _ST_ALPHA + 2 * buf + h
            rl = _ST_LFIN + 2 * buf + h
            acc = st_ref[ra:ra + 1, :] * acc_ref[rows, :] + pv
            acc_ref[rows, :] = acc
            o_ref[0, qi, rows, :] = acc / st_ref[rl:rl + 1, :]

    stage_a(0, s0_ref)

    def pair(jj, carry):
        j0 = 2 * jj
        stage_a(j0 + 1, s1_ref)
        stage_b(j0, s0_ref, p0_ref, 0)
        stage_c(jnp.maximum(j0 - 1, 0), p1_ref, 1)
        stage_a(jnp.minimum(j0 + 2, n_items - 1), s0_ref)
        stage_b(j0 + 1, s1_ref, p1_ref, 1)
        stage_c(j0, p0_ref, 0)
        return carry

    lax.fori_loop(0, n_items // 2, pair, 0)
    stage_c(n_items - 1, p1_ref, 1)


def _fox(qt, k, vt, f_row, f_col):
    bsz, nq, width, tq = qt.shape
    seq = k.shape[1]
    dh = width // FOX_HEADS
    pairs = FOX_HEADS // 2
    qi_tab = np.concatenate([np.full(i + 1, i, np.int32) for i in range(nq)])
    kj_tab = np.concatenate([np.arange(i + 1, dtype=np.int32) for i in range(nq)])
    n_items = len(qi_tab)
    assert n_items % 2 == 0, "the pipelined loop handles two items per trip"
    body = functools.partial(_fox_body, n_items=n_items, tq=tq, tk=tq, dh=dh)
    blk_t = pl.BlockSpec((1, nq, 2 * dh, tq), lambda b, p, qi, kj: (b, 0, p, 0))
    grid_spec = pltpu.PrefetchScalarGridSpec(
        num_scalar_prefetch=2,
        grid=(bsz, pairs),
        in_specs=[
            blk_t,
            pl.BlockSpec((1, seq, 2 * dh), lambda b, p, qi, kj: (b, 0, p)),
            blk_t,
            pl.BlockSpec((1, 1, nq, 2, tq), lambda b, p, qi, kj: (b, p, 0, 0, 0)),
            pl.BlockSpec((1, 1, seq, 2), lambda b, p, qi, kj: (b, p, 0, 0)),
        ],
        out_specs=blk_t,
        scratch_shapes=[pltpu.VMEM((2, tq, tq), F32),
                        pltpu.VMEM((2, tq, tq), F32),
                        pltpu.VMEM((2, tq, tq), BF16),
                        pltpu.VMEM((2, tq, tq), BF16),
                        pltpu.VMEM((16, tq), F32),
                        pltpu.VMEM((2 * dh, tq), F32),
                        pltpu.VMEM((2, tq, tq), F32)],
    )
    return pl.pallas_call(
        body,
        grid_spec=grid_spec,
        out_shape=jax.ShapeDtypeStruct((bsz, nq, width, tq), F32),
        compiler_params=pltpu.CompilerParams(
            dimension_semantics=("parallel", "parallel"),
            vmem_limit_bytes=VMEM_LIMIT),
        name="fox",
    )(jnp.asarray(qi_tab), jnp.asarray(kj_tab), qt, k, vt, f_row, f_col)


def _mid_body(x_ref, yl_ref, yf_ref, gnf_ref, wo_ref, g2_ref, wqt_ref, sk_ref,
              x1_ref, h2t_ref, s1_ref, s2_ref, *, half):
    yfn = _rms(yf_ref[0, 0].T, gnf_ref[...]).astype(BF16)
    mix = jnp.dot(yl_ref[...], wo_ref[0:half, :], preferred_element_type=F32)
    mix = mix + jnp.dot(yfn, wo_ref[half:2 * half, :], preferred_element_type=F32)
    x1 = x_ref[...] + mix
    x1_ref[...] = x1
    h2t = _rms(x1, g2_ref[...]).T.astype(BF16)
    h2t_ref[...] = h2t
    qpt = jnp.dot(wqt_ref[...], h2t, preferred_element_type=F32)
    for hp in range(2 * PEER_HEADS):
        qs = qpt[hp * LANES:(hp + 1) * LANES, :].astype(BF16)
        sc = jnp.dot(sk_ref[hp], qs, preferred_element_type=F32)
        if hp % 2 == 0:
            s1_ref[hp // 2] = sc
        else:
            s2_ref[hp // 2] = sc


def _mid(x2d, y_lru, y_fox_t, gn_fox, w_out, g2, wq_t, subkeys, *, tm):
    t, d = x2d.shape
    half = y_lru.shape[1]
    _, n_blk, _, t_blk = y_fox_t.shape
    per_blk = t_blk // tm
    body = functools.partial(_mid_body, half=half)
    tok = lambda i: (i, 0)
    const2 = lambda i: (0, 0)
    sc_shape = jax.ShapeDtypeStruct((PEER_HEADS, PEER_NKEYS, t), F32)
    sc_spec = pl.BlockSpec((PEER_HEADS, PEER_NKEYS, tm), lambda i: (0, 0, i))
    return pl.pallas_call(
        body,
        grid=(t // tm,),
        in_specs=[
            pl.BlockSpec((tm, d), tok),
            pl.BlockSpec((tm, half), tok),
            pl.BlockSpec((1, 1, half, tm),
                         lambda i: (i // (per_blk * n_blk), (i // per_blk) % n_blk, 0,
                                    i % per_blk)),
            pl.BlockSpec((1, half), const2),
            pl.BlockSpec(w_out.shape, const2),
            pl.BlockSpec((1, d), const2),
            pl.BlockSpec(wq_t.shape, const2),
            pl.BlockSpec(subkeys.shape, lambda i: (0, 0, 0)),
        ],
        out_specs=[pl.BlockSpec((tm, d), tok),
                   pl.BlockSpec((d, tm), lambda i: (0, i)),
                   sc_spec, sc_spec],
        out_shape=[jax.ShapeDtypeStruct((t, d), F32),
                   jax.ShapeDtypeStruct((d, t), BF16),
                   sc_shape, sc_shape],
        compiler_params=pltpu.CompilerParams(
            dimension_semantics=("parallel",),
            vmem_limit_bytes=VMEM_LIMIT),
        name="mid",
    )(x2d, y_lru, y_fox_t, gn_fox, w_out, g2, wq_t, subkeys)


_N_RANK = PEER_TOPK + 1
_CAND = [(a, b) for a in range(_N_RANK) for b in range(_N_RANK)
         if (a + 1) * (b + 1) <= _N_RANK]
_CAND_ROWS = -(-len(_CAND) // SUBLANES) * SUBLANES


def _top_values(x, n):
    vals = []
    for _ in range(n):
        mx = jnp.max(x, axis=0, keepdims=True)
        vals.append(mx)
        x = jnp.where(x == mx, NEG_INF, x)
    return vals


def _gates_body(s1_ref, s2_ref, th_ref, p1_ref, p2_ref, cand_ref):
    s1 = s1_ref[0]
    s2 = s2_ref[0]
    a = _top_values(s1, _N_RANK)
    b = _top_values(s2, _N_RANK)
    cand_ref[...] = jnp.full_like(cand_ref, NEG_INF)
    for r, (ia, ib) in enumerate(_CAND):
        cand_ref[r:r + 1, :] = a[ia] + b[ib]
    c = _top_values(cand_ref[...], _N_RANK)
    z = jnp.zeros_like(c[0])
    for r in range(PEER_TOPK):
        z = z + jnp.exp(c[r] - c[0])
    tau = 0.5 * (c[PEER_TOPK - 1] + c[PEER_TOPK])
    th_ref[0] = tau - s1
    p1_ref[0] = jnp.exp(s1 - a[0]) / z
    p2_ref[0] = jnp.exp(s2 - b[0])


def _gates(s1, s2, *, tm):
    heads, nk, t = s1.shape
    spec = pl.BlockSpec((1, nk, tm), lambda i, h: (h, 0, i))
    shape = jax.ShapeDtypeStruct(s1.shape, F32)
    return pl.pallas_call(
        _gates_body,
        grid=(t // tm, heads),
        in_specs=[spec, spec],
        out_specs=[spec, spec, spec],
        out_shape=[shape, shape, shape],
        scratch_shapes=[pltpu.VMEM((_CAND_ROWS, tm), F32)],
        compiler_params=pltpu.CompilerParams(
            dimension_semantics=("parallel", "parallel"),
            vmem_limit_bytes=VMEM_LIMIT),
        name="gates",
    )(s1, s2)


def _peer_body(h2t_ref, u_ref, vt_ref, s2_ref, p2_ref, th_ref, p1_ref, x1_ref, gf_ref,
               o_ref, acc_ref, pre_ref, wact_ref, *, rows_per_step):
    e = pl.program_id(1)

    @pl.when(e == 0)
    def _():
        acc_ref[...] = jnp.zeros_like(acc_ref)

    pre_ref[...] = jnp.dot(u_ref[...], h2t_ref[...], preferred_element_type=F32)

    def one_row(il, carry):
        i = e * rows_per_step + il
        w = jnp.zeros((PEER_NKEYS, pre_ref.shape[1]), F32)
        for h in range(PEER_HEADS):
            th = th_ref[h, pl.ds(i, 1), :]
            p1 = p1_ref[h, pl.ds(i, 1), :]
            w = w + jnp.where(s2_ref[h] >= th, p2_ref[h], 0.0) * p1
        off = pl.multiple_of(il * PEER_NKEYS, PEER_NKEYS)
        pre = pre_ref[pl.ds(off, PEER_NKEYS), :]
        wact_ref[pl.ds(off, PEER_NKEYS), :] = (jax.nn.gelu(pre) * w).astype(BF16)
        return carry

    lax.fori_loop(0, rows_per_step, one_row, 0)
    acc_ref[...] += jnp.dot(vt_ref[...], wact_ref[...], preferred_element_type=F32)

    @pl.when(e == pl.num_programs(1) - 1)
    def _():
        xo = x1_ref[...] + acc_ref[...].T
        o_ref[...] = _rms(xo, gf_ref[...])


def _peer(h2t, u, vt, s2, p2, th, p1, x1, gf, *, tm, te):
    d, t = h2t.shape
    n_exp = u.shape[0]
    rows_per_step = te // PEER_NKEYS
    body = functools.partial(_peer_body, rows_per_step=rows_per_step)
    hspec = pl.BlockSpec((PEER_HEADS, PEER_NKEYS, tm), lambda i, e: (0, 0, i))
    return pl.pallas_call(
        body,
        grid=(t // tm, n_exp // te),
        in_specs=[
            pl.BlockSpec((d, tm), lambda i, e: (0, i)),
            pl.BlockSpec((te, d), lambda i, e: (e, 0)),
            pl.BlockSpec((d, te), lambda i, e: (0, e)),
            hspec, hspec, hspec, hspec,
            pl.BlockSpec((tm, d), lambda i, e: (i, 0)),
            pl.BlockSpec((1, d), lambda i, e: (0, 0)),
        ],
        out_specs=pl.BlockSpec((tm, d), lambda i, e: (i, 0)),
        out_shape=jax.ShapeDtypeStruct((t, d), F32),
        scratch_shapes=[pltpu.VMEM((d, tm), F32),
                        pltpu.VMEM((te, tm), F32),
                        pltpu.VMEM((te, tm), BF16)],
        compiler_params=pltpu.CompilerParams(
            dimension_semantics=("parallel", "arbitrary"),
            vmem_limit_bytes=VMEM_LIMIT),
        name="peer",
    )(h2t, u, vt, s2, p2, th, p1, x1, gf)


def _block_diag(w):
    heads, dh, _ = w.shape
    eye = jnp.eye(heads, dtype=w.dtype)
    return jnp.einsum("hij,hg->higj", w, eye).reshape(heads * dh, heads * dh)


def _layer(x, norm1_g, w_in, conv_w, conv_b, lru_wa, lru_ba, lru_wx, lru_bx, lru_lambda,
           fox_bf, gn_lru_g, gn_fox_g, w_out, norm2_g, peer_wq, peer_subkeys, peer_u,
           peer_v, out_g):
    bsz, seq, d = x.shape
    width = lru_lambda.shape[0]
    row = lambda v: v.reshape(1, -1).astype(F32)

    w_main = w_in[:, :5 * width].astype(BF16)
    w_fl = jnp.pad(w_in[:, 5 * width:], ((0, 0), (0, LANES - FOX_HEADS))).astype(BF16)
    b_fl = jnp.pad(fox_bf.astype(F32), (0, LANES - FOX_HEADS)).reshape(1, LANES)
    t_blk = FOX_BLOCK
    xr, gr, qt, k, vt, f_t = _inproj(x, row(norm1_g), w_main, w_fl, b_fl, tm=t_blk)

    w_gate = jnp.concatenate([_block_diag(lru_wa), _block_diag(lru_wx)], axis=1).astype(BF16)
    b_gate = jnp.concatenate([lru_ba, lru_bx]).reshape(1, -1).astype(F32)
    y_lru = _rglru(xr, gr, conv_w.astype(F32), row(conv_b), w_gate, b_gate,
                   row(lru_lambda), row(gn_lru_g), ts=256)

    pairs = FOX_HEADS // 2
    f_pair = f_t.reshape(bsz, pairs, 2, seq)
    f_col = f_pair.transpose(0, 1, 3, 2)
    f_row = f_pair.reshape(bsz, pairs, 2, seq // t_blk, t_blk).transpose(0, 1, 3, 2, 4)
    y_fox_t = _fox(qt, k, vt, f_row, f_col)

    t = bsz * seq
    n_hp = 2 * PEER_HEADS
    subkeys = peer_subkeys.reshape(n_hp, PEER_NKEYS, -1).astype(BF16)
    x1, h2t, s1, s2 = _mid(x.reshape(t, d), y_lru.reshape(t, width), y_fox_t,
                           row(gn_fox_g), w_out.astype(BF16), row(norm2_g),
                           peer_wq.T.astype(BF16), subkeys, tm=256)

    th, p1, p2 = _gates(s1, s2, tm=256)
    out = _peer(h2t, peer_u.astype(BF16), peer_v.T.astype(BF16), s2, p2, th, p1, x1,
                row(out_g), tm=512, te=1024)
    return out.reshape(bsz, seq, d)


def kernel(x, norm1_g, w_in, conv_w, conv_b, lru_wa, lru_ba, lru_wx, lru_bx, lru_lambda,
           fox_bf, gn_lru_g, gn_fox_g, w_out, norm2_g, peer_wq, peer_subkeys, peer_u,
           peer_v, final_g):
    depth = w_in.shape[0]
    assert depth == 1, "the fused final RMSNorm assumes a single trunk layer"
    return _layer(x, norm1_g[0], w_in[0], conv_w[0], conv_b[0], lru_wa[0], lru_ba[0],
                  lru_wx[0], lru_bx[0], lru_lambda[0], fox_bf[0], gn_lru_g[0], gn_fox_g[0],
                  w_out[0], norm2_g[0], peer_wq[0], peer_subkeys[0], peer_u[0], peer_v[0],
                  final_g)
```

```python
import functools

import jax
import jax.numpy as jnp
import numpy as np
from jax import lax
from jax.experimental import pallas as pl
from jax.experimental.pallas import tpu as pltpu

F32 = jnp.float32
BF16 = jnp.bfloat16

RMS_EPS = 1e-6
LRU_C = 8.0
CONV_WIDTH = 4
LRU_HEADS = 8
FOX_HEADS = 8
PEER_HEADS = 8
PEER_NKEYS = 128
PEER_TOPK = 16
FOX_BLOCK = 512

LANES = 128
SUBLANES = 8
VMEM_LIMIT = 56 * 1024 * 1024

NEG_INF = float("-inf")
LOG2E = 1.4426950408889634
_GELU_K = 0.7978845608028654
BF16_ROWS = 16


def _rms(x, g):
    return x * lax.rsqrt(jnp.mean(x * x, axis=-1, keepdims=True) + RMS_EPS) * g


def _gelu_tanh(x):
    inner = x * (_GELU_K + (_GELU_K * 0.044715) * (x * x))
    return x * (0.5 + 0.5 * jnp.tanh(inner))


def _inproj_body(x_ref, g_ref, w_ref, wfl_ref, bf_ref,
                 xr_ref, gr_ref, qt_ref, k_ref, vt_ref, ft_ref, carry_ref,
                 *, tm, width, q_scale):
    s = pl.program_id(1)

    @pl.when(s == 0)
    def _():
        carry_ref[...] = jnp.zeros_like(carry_ref)

    hb = _rms(x_ref[0], g_ref[...]).astype(BF16)

    def piece(i):
        return jnp.dot(hb, w_ref[:, i * width:(i + 1) * width],
                       preferred_element_type=F32)

    xr_ref[0] = piece(0)
    gr_ref[0] = piece(1)
    qt_ref[0, 0] = (piece(2) * (q_scale * LOG2E)).T.astype(BF16)
    k_ref[0] = piece(3).astype(BF16)
    vt_ref[0, 0] = piece(4).T.astype(BF16)

    fl = jnp.dot(hb, wfl_ref[...], preferred_element_type=F32)
    c = jax.nn.log_sigmoid(fl + bf_ref[...]) * LOG2E
    row = lax.broadcasted_iota(jnp.int32, c.shape, 0)
    sh = 1
    while sh < tm:
        c = c + jnp.where(row >= sh, pltpu.roll(c, sh, axis=0), 0.0)
        sh *= 2
    f = c + carry_ref[...]
    carry_ref[...] = f[tm - 1:tm, :]
    ft_ref[0] = f.T[0:FOX_HEADS, :]


def _inproj(x, g, w_main, w_fl, b_fl, *, tm):
    bsz, seq, d = x.shape
    width = w_main.shape[1] // 5
    body = functools.partial(_inproj_body, tm=tm, width=width,
                             q_scale=(width // FOX_HEADS) ** -0.5)
    tok = lambda b, s: (b, s, 0)
    const = lambda b, s: (0, 0)
    big = jax.ShapeDtypeStruct((bsz, seq, width), F32)
    bigb = jax.ShapeDtypeStruct((bsz, seq, width), BF16)
    blocked_t = jax.ShapeDtypeStruct((bsz, seq // tm, width, tm), BF16)
    return pl.pallas_call(
        body,
        grid=(bsz, seq // tm),
        in_specs=[
            pl.BlockSpec((1, tm, d), tok),
            pl.BlockSpec((1, d), const),
            pl.BlockSpec(w_main.shape, const),
            pl.BlockSpec(w_fl.shape, const),
            pl.BlockSpec((1, LANES), const),
        ],
        out_specs=[pl.BlockSpec((1, tm, width), tok),
                   pl.BlockSpec((1, tm, width), tok),
                   pl.BlockSpec((1, 1, width, tm), lambda b, s: (b, s, 0, 0)),
                   pl.BlockSpec((1, tm, width), tok),
                   pl.BlockSpec((1, 1, width, tm), lambda b, s: (b, s, 0, 0)),
                   pl.BlockSpec((1, FOX_HEADS, tm), lambda b, s: (b, 0, s))],
        out_shape=[big, big, blocked_t, bigb, blocked_t,
                   jax.ShapeDtypeStruct((bsz, FOX_HEADS, seq), F32)],
        scratch_shapes=[pltpu.VMEM((1, LANES), F32)],
        compiler_params=pltpu.CompilerParams(
            dimension_semantics=("parallel", "arbitrary"),
            vmem_limit_bytes=VMEM_LIMIT),
        name="inproj",
    )(x, g, w_main, w_fl, b_fl)


def _rglru_body(xr_ref, gr_ref, cw_ref, cb_ref, wg_ref, bg_ref, lam_ref, gn_ref,
                y_ref, xbuf_ref, hc_ref, *, ts, width):
    s = pl.program_id(1)

    @pl.when(s == 0)
    def _():
        xbuf_ref[0:SUBLANES, :] = jnp.zeros((SUBLANES, width), F32)
        hc_ref[...] = jnp.zeros_like(hc_ref)

    x = xr_ref[0]
    xbuf_ref[SUBLANES:SUBLANES + ts, :] = x
    cw = cw_ref[...]
    conv = cb_ref[...] + cw[CONV_WIDTH - 1:CONV_WIDTH, :] * x
    for back in range(1, CONV_WIDTH):
        k = CONV_WIDTH - 1 - back
        conv = conv + cw[k:k + 1, :] * xbuf_ref[SUBLANES - back:SUBLANES - back + ts, :]
    xbuf_ref[0:SUBLANES, :] = x[ts - SUBLANES:ts, :]

    gates = jnp.dot(conv.astype(BF16), wg_ref[...],
                    preferred_element_type=F32) + bg_ref[...]
    r = jax.nn.sigmoid(gates[:, :width])
    i = jax.nn.sigmoid(gates[:, width:])
    log_a = (-LRU_C) * r * jax.nn.softplus(-lam_ref[...])
    a = jnp.exp(log_a)
    u = jnp.sqrt(1.0 - a * a) * (i * conv)

    row = lax.broadcasted_iota(jnp.int32, a.shape, 0)
    sh = 1
    while sh < ts:
        keep = row >= sh
        a_s = jnp.where(keep, pltpu.roll(a, sh, axis=0), 1.0)
        u_s = jnp.where(keep, pltpu.roll(u, sh, axis=0), 0.0)
        u = a * u_s + u
        a = a * a_s
        sh *= 2
    h = u + a * hc_ref[...]
    hc_ref[...] = h[ts - 1:ts, :]

    y = h * jax.nn.gelu(gr_ref[0])
    y_ref[0] = _rms(y, gn_ref[...]).astype(BF16)


def _rglru(xr, gr, conv_w, conv_b, w_gate, b_gate, lam, gn, *, ts):
    bsz, seq, width = xr.shape
    body = functools.partial(_rglru_body, ts=ts, width=width)
    tok = lambda b, s: (b, s, 0)
    const = lambda b, s: (0, 0)
    return pl.pallas_call(
        body,
        grid=(bsz, seq // ts),
        in_specs=[
            pl.BlockSpec((1, ts, width), tok),
            pl.BlockSpec((1, ts, width), tok),
            pl.BlockSpec(conv_w.shape, const),
            pl.BlockSpec((1, width), const),
            pl.BlockSpec(w_gate.shape, const),
            pl.BlockSpec((1, 2 * width), const),
            pl.BlockSpec((1, width), const),
            pl.BlockSpec((1, width), const),
        ],
        out_specs=pl.BlockSpec((1, ts, width), tok),
        out_shape=jax.ShapeDtypeStruct((bsz, seq, width), BF16),
        scratch_shapes=[pltpu.VMEM((ts + SUBLANES, width), F32),
                        pltpu.VMEM((1, width), F32)],
        compiler_params=pltpu.CompilerParams(
            dimension_semantics=("parallel", "arbitrary"),
            vmem_limit_bytes=VMEM_LIMIT),
        name="rglru",
    )(xr, gr, conv_w, conv_b, w_gate, b_gate, lam, gn)


_ST_M, _ST_L, _ST_ALPHA, _ST_LFIN = 0, 2, 4, 8


def _fox_body(qi_ref, kj_ref, qt_ref, k_ref, vt_ref, fq_ref, fk_ref, o_ref,
              s0_ref, s1_ref, p0_ref, p1_ref, st_ref, acc_ref, mb_ref,
              *, n_items, tq, tk, dh):
    row_t = lax.broadcasted_iota(jnp.int32, (tk, tq), 0)
    col_t = lax.broadcasted_iota(jnp.int32, (tk, tq), 1)
    mb_ref[0] = jnp.zeros((tk, tq), F32)
    mb_ref[1] = jnp.where(row_t <= col_t, 0.0, NEG_INF)
    st_ref[...] = jnp.ones_like(st_ref)
    acc_ref[...] = jnp.zeros_like(acc_ref)
    p1_ref[...] = jnp.zeros_like(p1_ref)
    head_row = lax.broadcasted_iota(jnp.int32, (2 * dh, 1), 0)

    def stage_a(j, s_ref):
        qt = qt_ref[0, qi_ref[j]]
        kb = k_ref[0, pl.ds(pl.multiple_of(kj_ref[j] * tk, tk), tk), :]
        for h in range(2):
            in_head = (head_row >= h * dh) & (head_row < (h + 1) * dh)
            qm = jnp.where(in_head, qt, jnp.zeros_like(qt))
            s_ref[h] = jnp.dot(kb, qm, preferred_element_type=F32)

    def stage_b(j, s_ref, p_ref, buf):
        qi = qi_ref[j]
        kj = kj_ref[j]
        mb = mb_ref[(kj == qi).astype(jnp.int32)]
        fq2 = fq_ref[0, 0, qi]
        for h in range(2):
            fk = fk_ref[0, 0, pl.ds(pl.multiple_of(kj * tk, tk), tk), h:h + 1]
            fq = fq2[h:h + 1, :]
            z = s_ref[h] - fk + mb
            mz = jnp.max(z, axis=0, keepdims=True) + fq
            m_prev = jnp.where(kj == 0, NEG_INF, st_ref[_ST_M + h:_ST_M + h + 1, :])
            m_new = jnp.maximum(m_prev, mz)
            alpha = jnp.exp2(m_prev - m_new)
            p = jnp.exp2(z - (m_new - fq))
            l_new = (alpha * st_ref[_ST_L + h:_ST_L + h + 1, :]
                     + jnp.sum(p, axis=0, keepdims=True))
            p_ref[h] = p.astype(BF16)
            st_ref[_ST_M + h:_ST_M + h + 1, :] = m_new
            st_ref[_ST_L + h:_ST_L + h + 1, :] = l_new
            ra = _ST_ALPHA + 2 * buf + h
            st_ref[ra:ra + 1, :] = alpha
            rl = _ST_LFIN + 2 * buf + h
            st_ref[rl:rl + 1, :] = l_new

    def stage_c(j, p_ref, buf):
        qi = qi_ref[j]
        vt = vt_ref[0, kj_ref[j]]
        for h in range(2):
            rows = slice(h * dh, (h + 1) * dh)
            pv = jnp.dot(vt[rows, :], p_ref[h], preferred_element_type=F32)
            ra = _ST_ALPHA + 2 * buf + h
            rl = _ST_LFIN + 2 * buf + h
            acc = st_ref[ra:ra + 1, :] * acc_ref[rows, :] + pv
            acc_ref[rows, :] = acc
            o_ref[0, qi, rows, :] = acc / st_ref[rl:rl + 1, :]

    stage_a(0, s0_ref)

    def pair(jj, carry):
        j0 = 2 * jj
        stage_a(j0 + 1, s1_ref)
        stage_b(j0, s0_ref, p0_ref, 0)
        stage_c(jnp.maximum(j0 - 1, 0), p1_ref, 1)
        stage_a(jnp.minimum(j0 + 2, n_items - 1), s0_ref)
        stage_b(j0 + 1, s1_ref, p1_ref, 1)
        stage_c(j0, p0_ref, 0)
        return carry

    lax.fori_loop(0, n_items // 2, pair, 0)
    stage_c(n_items - 1, p1_ref, 1)


def _fox(qt, k, vt, f_row, f_col):
    bsz, nq, width, tq = qt.shape
    seq = k.shape[1]
    dh = width // FOX_HEADS
    pairs = FOX_HEADS // 2
    qi_tab = np.concatenate([np.full(i + 1, i, np.int32) for i in range(nq)])
    kj_tab = np.concatenate([np.arange(i + 1, dtype=np.int32) for i in range(nq)])
    n_items = len(qi_tab)
    assert n_items % 2 == 0, "the pipelined loop handles two items per trip"
    body = functools.partial(_fox_body, n_items=n_items, tq=tq, tk=tq, dh=dh)
    blk_t = pl.BlockSpec((1, nq, 2 * dh, tq), lambda b, p, qi, kj: (b, 0, p, 0))
    grid_spec = pltpu.PrefetchScalarGridSpec(
        num_scalar_prefetch=2,
        grid=(bsz, pairs),
        in_specs=[
            blk_t,
            pl.BlockSpec((1, seq, 2 * dh), lambda b, p, qi, kj: (b, 0, p)),
            blk_t,
            pl.BlockSpec((1, 1, nq, 2, tq), lambda b, p, qi, kj: (b, p, 0, 0, 0)),
            pl.BlockSpec((1, 1, seq, 2), lambda b, p, qi, kj: (b, p, 0, 0)),
        ],
        out_specs=blk_t,
        scratch_shapes=[pltpu.VMEM((2, tq, tq), F32),
                        pltpu.VMEM((2, tq, tq), F32),
                        pltpu.VMEM((2, tq, tq), BF16),
                        pltpu.VMEM((2, tq, tq), BF16),
                        pltpu.VMEM((16, tq), F32),
                        pltpu.VMEM((2 * dh, tq), F32),
                        pltpu.VMEM((2, tq, tq), F32)],
    )
    return pl.pallas_call(
        body,
        grid_spec=grid_spec,
        out_shape=jax.ShapeDtypeStruct((bsz, nq, width, tq), F32),
        compiler_params=pltpu.CompilerParams(
            dimension_semantics=("parallel", "parallel"),
            vmem_limit_bytes=VMEM_LIMIT),
        name="fox",
    )(jnp.asarray(qi_tab), jnp.asarray(kj_tab), qt, k, vt, f_row, f_col)


def _mid_body(x_ref, yl_ref, yf_ref, gnf_ref, wo_ref, g2_ref, wqt_ref, sk_ref,
              x1_ref, h2t_ref, s1_ref, s2_ref, *, half):
    yfn = _rms(yf_ref[0, 0].T, gnf_ref[...]).astype(BF16)
    mix = jnp.dot(yl_ref[...], wo_ref[0:half, :], preferred_element_type=F32)
    mix = mix + jnp.dot(yfn, wo_ref[half:2 * half, :], preferred_element_type=F32)
    x1 = x_ref[...] + mix
    x1_ref[...] = x1
    h2t = _rms(x1, g2_ref[...]).T.astype(BF16)
    h2t_ref[...] = h2t
    qpt = jnp.dot(wqt_ref[...], h2t, preferred_element_type=F32)
    for hp in range(2 * PEER_HEADS):
        qs = qpt[hp * LANES:(hp + 1) * LANES, :].astype(BF16)
        sc = jnp.dot(sk_ref[hp], qs, preferred_element_type=F32)
        if hp % 2 == 0:
            s1_ref[hp // 2] = sc
        else:
            s2_ref[hp // 2] = sc


def _mid(x2d, y_lru, y_fox_t, gn_fox, w_out, g2, wq_t, subkeys, *, tm):
    t, d = x2d.shape
    half = y_lru.shape[1]
    _, n_blk, _, t_blk = y_fox_t.shape
    per_blk = t_blk // tm
    body = functools.partial(_mid_body, half=half)
    tok = lambda i: (i, 0)
    const2 = lambda i: (0, 0)
    sc_shape = jax.ShapeDtypeStruct((PEER_HEADS, PEER_NKEYS, t), F32)
    sc_spec = pl.BlockSpec((PEER_HEADS, PEER_NKEYS, tm), lambda i: (0, 0, i))
    return pl.pallas_call(
        body,
        grid=(t // tm,),
        in_specs=[
            pl.BlockSpec((tm, d), tok),
            pl.BlockSpec((tm, half), tok),
            pl.BlockSpec((1, 1, half, tm),
                         lambda i: (i // (per_blk * n_blk), (i // per_blk) % n_blk, 0,
                                    i % per_blk)),
            pl.BlockSpec((1, half), const2),
            pl.BlockSpec(w_out.shape, const2),
            pl.BlockSpec((1, d), const2),
            pl.BlockSpec(wq_t.shape, const2),
            pl.BlockSpec(subkeys.shape, lambda i: (0, 0, 0)),
        ],
        out_specs=[pl.BlockSpec((tm, d), tok),
                   pl.BlockSpec((d, tm), lambda i: (0, i)),
                   sc_spec, sc_spec],
        out_shape=[jax.ShapeDtypeStruct((t, d), F32),
                   jax.ShapeDtypeStruct((d, t), BF16),
                   sc_shape, sc_shape],
        compiler_params=pltpu.CompilerParams(
            dimension_semantics=("parallel",),
            vmem_limit_bytes=VMEM_LIMIT),
        name="mid",
    )(x2d, y_lru, y_fox_t, gn_fox, w_out, g2, wq_t, subkeys)


_CAND = [(a, b) for a in range(PEER_TOPK) for b in range(PEER_TOPK)
         if (a + 1) * (b + 1) <= PEER_TOPK]
_CAND_ROWS = -(-len(_CAND) // SUBLANES) * SUBLANES


def _bf16_pair_word(x):
    bits = pltpu.bitcast(x.astype(BF16).astype(F32), jnp.uint32)
    return bits | (bits >> 16)


def _top_values(x, n):
    vals = []
    for _ in range(n):
        mx = jnp.max(x, axis=0, keepdims=True)
        vals.append(mx)
        x = jnp.where(x == mx, NEG_INF, x)
    return vals


def _gates_body(s1_ref, s2_ref, n_ref, p1_ref, rank_ref, p2_ref, cand_ref):
    s1 = s1_ref[0]
    s2 = s2_ref[0]
    a = _top_values(s1, PEER_TOPK)
    b = _top_values(s2, PEER_TOPK)
    cand_ref[...] = jnp.full_like(cand_ref, NEG_INF)
    for r, (ia, ib) in enumerate(_CAND):
        cand_ref[r:r + 1, :] = a[ia] + b[ib]
    c = _top_values(cand_ref[...], PEER_TOPK)
    z = jnp.zeros_like(c[0])
    for r in range(PEER_TOPK):
        z = z + jnp.exp(c[r] - c[0])
    tau = c[PEER_TOPK - 1]
    n = jnp.zeros_like(s1)
    rank = jnp.zeros_like(s2)
    for r in range(PEER_TOPK):
        n = n + jnp.where(s1 + b[r] >= tau, 1.0, 0.0)
        rank = rank + jnp.where(b[r] > s2, 1.0, 0.0)
    n_ref[0] = _bf16_pair_word(n)
    p1_ref[0] = _bf16_pair_word(jnp.exp(s1 - a[0]) / z)
    rank_ref[0] = rank.astype(BF16)
    p2_ref[0] = jnp.exp(s2 - b[0]).astype(BF16)


def _gates(s1, s2, *, tm):
    heads, nk, t = s1.shape
    spec = pl.BlockSpec((1, nk, tm), lambda i, h: (h, 0, i))
    shape = jax.ShapeDtypeStruct(s1.shape, jnp.uint32)
    shape_b = jax.ShapeDtypeStruct(s1.shape, BF16)
    return pl.pallas_call(
        _gates_body,
        grid=(t // tm, heads),
        in_specs=[spec, spec],
        out_specs=[spec, spec, spec, spec],
        out_shape=[shape, shape, shape_b, shape_b],
        scratch_shapes=[pltpu.VMEM((_CAND_ROWS, tm), F32)],
        compiler_params=pltpu.CompilerParams(
            dimension_semantics=("parallel", "parallel"),
            vmem_limit_bytes=VMEM_LIMIT),
        name="gates",
    )(s1, s2)


def _peer_body(h2_ref, u_ref, vt_ref, rank_ref, p2_ref, n_ref, p1_ref, x1_ref, gf_ref,
               o_ref, pre_ref, acc_ref, *, per_tile, rows_per_item):
    s = pl.program_id(0)
    n = s - 1

    @pl.when(s == 0)
    def _():
        pre_ref[...] = jnp.zeros_like(pre_ref)
        acc_ref[...] = jnp.zeros_like(acc_ref)

    pre_next = jnp.dot(u_ref[...], h2_ref[...], preferred_element_type=F32)

    tm = pre_ref.shape[1]
    groups = PEER_NKEYS // BF16_ROWS

    def packed_row(ref, h, r):
        words = jnp.broadcast_to(ref[h, r:r + 1, :], (SUBLANES, tm))
        return pltpu.bitcast(words, BF16)[None]

    chunks = []
    for r in range(rows_per_item):
        w = jnp.zeros((groups, BF16_ROWS, tm), BF16)
        for h in range(PEER_HEADS):
            rank = rank_ref[h].reshape(groups, BF16_ROWS, tm)
            p2 = p2_ref[h].reshape(groups, BF16_ROWS, tm)
            sel = jnp.where(rank < packed_row(n_ref, h, r), p2, jnp.zeros_like(p2))
            w = w + sel * packed_row(p1_ref, h, r)
        rows = slice(r * PEER_NKEYS, (r + 1) * PEER_NKEYS)
        act = _gelu_tanh(pre_ref[rows, :]) * w.reshape(PEER_NKEYS, tm).astype(F32)
        chunks.append(act.astype(BF16))
    wact = jnp.concatenate(chunks, axis=0)

    pv = jnp.dot(vt_ref[...], wact, preferred_element_type=F32)
    first = (n % per_tile) == 0
    acc_ref[...] = jnp.where(first, pv, acc_ref[...] + pv)
    pre_ref[...] = pre_next

    @pl.when((n >= 0) & (n % per_tile == per_tile - 1))
    def _():
        xo = x1_ref[...] + acc_ref[...].T
        o_ref[...] = _rms(xo, gf_ref[...])


def _peer(h2t, u, vt, rank, p2, n_word, p1_word, x1, gf, *, tm, te):
    d, t = h2t.shape
    n_exp = u.shape[0]
    per_tile = n_exp // te
    rows_per_item = te // PEER_NKEYS
    n_items = (t // tm) * per_tile
    body = functools.partial(_peer_body, per_tile=per_tile, rows_per_item=rows_per_item)

    def item(s, off):
        return jnp.clip(s - 1 + off, 0, n_items - 1)

    tile = lambda s, off: item(s, off) // per_tile
    blk = lambda s, off: item(s, off) % per_tile
    full_spec = pl.BlockSpec((PEER_HEADS, PEER_NKEYS, tm), lambda s: (0, 0, tile(s, 0)))
    rows_spec = pl.BlockSpec((PEER_HEADS, rows_per_item, tm),
                             lambda s: (0, blk(s, 0), tile(s, 0)))
    tok_spec = pl.BlockSpec((tm, d), lambda s: (tile(s, 0), 0))
    return pl.pallas_call(
        body,
        grid=(n_items + 1,),
        in_specs=[
            pl.BlockSpec((d, tm), lambda s: (0, tile(s, 1))),
            pl.BlockSpec((te, d), lambda s: (blk(s, 1), 0)),
            pl.BlockSpec((d, te), lambda s: (0, blk(s, 0))),
            full_spec, full_spec, rows_spec, rows_spec, tok_spec,
            pl.BlockSpec((1, d), lambda s: (0, 0)),
        ],
        out_specs=tok_spec,
        out_shape=jax.ShapeDtypeStruct((t, d), F32),
        scratch_shapes=[pltpu.VMEM((te, tm), F32), pltpu.VMEM((d, tm), F32)],
        compiler_params=pltpu.CompilerParams(
            dimension_semantics=("arbitrary",),
            vmem_limit_bytes=VMEM_LIMIT),
        name="peer",
    )(h2t, u, vt, rank, p2, n_word, p1_word, x1, gf)


def _block_diag(w):
    heads, dh, _ = w.shape
    eye = jnp.eye(heads, dtype=w.dtype)
    return jnp.einsum("hij,hg->higj", w, eye).reshape(heads * dh, heads * dh)


def _layer(x, norm1_g, w_in, conv_w, conv_b, lru_wa, lru_ba, lru_wx, lru_bx, lru_lambda,
           fox_bf, gn_lru_g, gn_fox_g, w_out, norm2_g, peer_wq, peer_subkeys, peer_u,
           peer_v, out_g):
    bsz, seq, d = x.shape
    width = lru_lambda.shape[0]
    row = lambda v: v.reshape(1, -1).astype(F32)

    w_main = w_in[:, :5 * width].astype(BF16)
    w_fl = jnp.pad(w_in[:, 5 * width:], ((0, 0), (0, LANES - FOX_HEADS))).astype(BF16)
    b_fl = jnp.pad(fox_bf.astype(F32), (0, LANES - FOX_HEADS)).reshape(1, LANES)
    t_blk = FOX_BLOCK
    xr, gr, qt, k, vt, f_t = _inproj(x, row(norm1_g), w_main, w_fl, b_fl, tm=t_blk)

    w_gate = jnp.concatenate([_block_diag(lru_wa), _block_diag(lru_wx)], axis=1).astype(BF16)
    b_gate = jnp.concatenate([lru_ba, lru_bx]).reshape(1, -1).astype(F32)
    y_lru = _rglru(xr, gr, conv_w.astype(F32), row(conv_b), w_gate, b_gate,
                   row(lru_lambda), row(gn_lru_g), ts=256)

    pairs = FOX_HEADS // 2
    f_pair = f_t.reshape(bsz, pairs, 2, seq)
    f_col = f_pair.transpose(0, 1, 3, 2)
    f_row = f_pair.reshape(bsz, pairs, 2, seq // t_blk, t_blk).transpose(0, 1, 3, 2, 4)
    y_fox_t = _fox(qt, k, vt, f_row, f_col)

    t = bsz * seq
    n_hp = 2 * PEER_HEADS
    subkeys = peer_subkeys.reshape(n_hp, PEER_NKEYS, -1).astype(BF16)
    x1, h2t, s1, s2 = _mid(x.reshape(t, d), y_lru.reshape(t, width), y_fox_t,
                           row(gn_fox_g), w_out.astype(BF16), row(norm2_g),
                           peer_wq.T.astype(BF16), subkeys, tm=256)

    n_sel, p1, rank2, p2 = _gates(s1, s2, tm=256)
    out = _peer(h2t, peer_u.astype(BF16), peer_v.T.astype(BF16), rank2, p2, n_sel, p1, x1,
                row(out_g), tm=512, te=1024)
    return out.reshape(bsz, seq, d)


def kernel(x, norm1_g, w_in, conv_w, conv_b, lru_wa, lru_ba, lru_wx, lru_bx, lru_lambda,
           fox_bf, gn_lru_g, gn_fox_g, w_out, norm2_g, peer_wq, peer_subkeys, peer_u,
           peer_v, final_g):
    depth = w_in.shape[0]
    assert depth == 1, "the fused final RMSNorm assumes a single trunk layer"
    return _layer(x, norm1_g[0], w_in[0], conv_w[0], conv_b[0], lru_wa[0], lru_ba[0],
                  lru_wx[0], lru_bx[0], lru_lambda[0], fox_bf[0], gn_lru_g[0], gn_fox_g[0],
                  w_out[0], norm2_g[0], peer_wq[0], peer_subkeys[0], peer_u[0], peer_v[0],
                  final_g)
```

```python
import functools

import jax
import jax.numpy as jnp
import numpy as np
from jax import lax
from jax.experimental import pallas as pl
from jax.experimental.pallas import tpu as pltpu

F32 = jnp.float32
BF16 = jnp.bfloat16

RMS_EPS = 1e-6
LRU_C = 8.0
CONV_WIDTH = 4
LRU_HEADS = 8
FOX_HEADS = 8
PEER_HEADS = 8
PEER_NKEYS = 128
PEER_TOPK = 16
FOX_BLOCK = 512

LANES = 128
SUBLANES = 8
VMEM_LIMIT = 56 * 1024 * 1024

NEG_INF = float("-inf")
LOG2E = 1.4426950408889634
_GELU_K = 0.7978845608028654
BF16_ROWS = 16


def _rms(x, g):
    return x * lax.rsqrt(jnp.mean(x * x, axis=-1, keepdims=True) + RMS_EPS) * g


def _gelu_tanh_twice(x):
    inner = x * (_GELU_K + (_GELU_K * 0.044715) * (x * x))
    return x + x * jnp.tanh(inner)


def _inproj_body(x_ref, g_ref, w_ref, wfl_ref, bf_ref,
                 xr_ref, gr_ref, qt_ref, k_ref, vt_ref, ft_ref, carry_ref,
                 *, tm, width, q_scale):
    s = pl.program_id(1)

    @pl.when(s == 0)
    def _():
        carry_ref[...] = jnp.zeros_like(carry_ref)

    hb = _rms(x_ref[0], g_ref[...]).astype(BF16)

    def piece(i):
        return jnp.dot(hb, w_ref[:, i * width:(i + 1) * width],
                       preferred_element_type=F32)

    xr_ref[0] = piece(0)
    gr_ref[0] = piece(1)
    qt_ref[0, 0] = (piece(2) * (q_scale * LOG2E)).T.astype(BF16)
    k_ref[0] = piece(3).astype(BF16)
    vt_ref[0, 0] = piece(4).T.astype(BF16)

    fl = jnp.dot(hb, wfl_ref[...], preferred_element_type=F32)
    c = jax.nn.log_sigmoid(fl + bf_ref[...]) * LOG2E
    row = lax.broadcasted_iota(jnp.int32, c.shape, 0)
    sh = 1
    while sh < tm:
        c = c + jnp.where(row >= sh, pltpu.roll(c, sh, axis=0), 0.0)
        sh *= 2
    f = c + carry_ref[...]
    carry_ref[...] = f[tm - 1:tm, :]
    ft_ref[0] = f.T[0:FOX_HEADS, :]


def _inproj(x, g, w_main, w_fl, b_fl, *, tm):
    bsz, seq, d = x.shape
    width = w_main.shape[1] // 5
    body = functools.partial(_inproj_body, tm=tm, width=width,
                             q_scale=(width // FOX_HEADS) ** -0.5)
    tok = lambda b, s: (b, s, 0)
    const = lambda b, s: (0, 0)
    big = jax.ShapeDtypeStruct((bsz, seq, width), F32)
    bigb = jax.ShapeDtypeStruct((bsz, seq, width), BF16)
    blocked_t = jax.ShapeDtypeStruct((bsz, seq // tm, width, tm), BF16)
    return pl.pallas_call(
        body,
        grid=(bsz, seq // tm),
        in_specs=[
            pl.BlockSpec((1, tm, d), tok),
            pl.BlockSpec((1, d), const),
            pl.BlockSpec(w_main.shape, const),
            pl.BlockSpec(w_fl.shape, const),
            pl.BlockSpec((1, LANES), const),
        ],
        out_specs=[pl.BlockSpec((1, tm, width), tok),
                   pl.BlockSpec((1, tm, width), tok),
                   pl.BlockSpec((1, 1, width, tm), lambda b, s: (b, s, 0, 0)),
                   pl.BlockSpec((1, tm, width), tok),
                   pl.BlockSpec((1, 1, width, tm), lambda b, s: (b, s, 0, 0)),
                   pl.BlockSpec((1, FOX_HEADS, tm), lambda b, s: (b, 0, s))],
        out_shape=[big, big, blocked_t, bigb, blocked_t,
                   jax.ShapeDtypeStruct((bsz, FOX_HEADS, seq), F32)],
        scratch_shapes=[pltpu.VMEM((1, LANES), F32)],
        compiler_params=pltpu.CompilerParams(
            dimension_semantics=("parallel", "arbitrary"),
            vmem_limit_bytes=VMEM_LIMIT),
        name="inproj",
    )(x, g, w_main, w_fl, b_fl)


def _rglru_body(xr_ref, gr_ref, cw_ref, cb_ref, wg_ref, bg_ref, lam_ref, gn_ref,
                y_ref, xbuf_ref, hc_ref, *, ts, width):
    s = pl.program_id(1)

    @pl.when(s == 0)
    def _():
        xbuf_ref[0:SUBLANES, :] = jnp.zeros((SUBLANES, width), F32)
        hc_ref[...] = jnp.zeros_like(hc_ref)

    x = xr_ref[0]
    xbuf_ref[SUBLANES:SUBLANES + ts, :] = x
    cw = cw_ref[...]
    conv = cb_ref[...] + cw[CONV_WIDTH - 1:CONV_WIDTH, :] * x
    for back in range(1, CONV_WIDTH):
        k = CONV_WIDTH - 1 - back
        conv = conv + cw[k:k + 1, :] * xbuf_ref[SUBLANES - back:SUBLANES - back + ts, :]
    xbuf_ref[0:SUBLANES, :] = x[ts - SUBLANES:ts, :]

    gates = jnp.dot(conv.astype(BF16), wg_ref[...],
                    preferred_element_type=F32) + bg_ref[...]
    r = jax.nn.sigmoid(gates[:, :width])
    i = jax.nn.sigmoid(gates[:, width:])
    log_a = (-LRU_C) * r * jax.nn.softplus(-lam_ref[...])
    a = jnp.exp(log_a)
    u = jnp.sqrt(1.0 - a * a) * (i * conv)

    row = lax.broadcasted_iota(jnp.int32, a.shape, 0)
    sh = 1
    while sh < ts:
        keep = row >= sh
        a_s = jnp.where(keep, pltpu.roll(a, sh, axis=0), 1.0)
        u_s = jnp.where(keep, pltpu.roll(u, sh, axis=0), 0.0)
        u = a * u_s + u
        a = a * a_s
        sh *= 2
    h = u + a * hc_ref[...]
    hc_ref[...] = h[ts - 1:ts, :]

    y = h * jax.nn.gelu(gr_ref[0])
    y_ref[0] = _rms(y, gn_ref[...]).astype(BF16)


def _rglru(xr, gr, conv_w, conv_b, w_gate, b_gate, lam, gn, *, ts):
    bsz, seq, width = xr.shape
    body = functools.partial(_rglru_body, ts=ts, width=width)
    tok = lambda b, s: (b, s, 0)
    const = lambda b, s: (0, 0)
    return pl.pallas_call(
        body,
        grid=(bsz, seq // ts),
        in_specs=[
            pl.BlockSpec((1, ts, width), tok),
            pl.BlockSpec((1, ts, width), tok),
            pl.BlockSpec(conv_w.shape, const),
            pl.BlockSpec((1, width), const),
            pl.BlockSpec(w_gate.shape, const),
            pl.BlockSpec((1, 2 * width), const),
            pl.BlockSpec((1, width), const),
            pl.BlockSpec((1, width), const),
        ],
        out_specs=pl.BlockSpec((1, ts, width), tok),
        out_shape=jax.ShapeDtypeStruct((bsz, seq, width), BF16),
        scratch_shapes=[pltpu.VMEM((ts + SUBLANES, width), F32),
                        pltpu.VMEM((1, width), F32)],
        compiler_params=pltpu.CompilerParams(
            dimension_semantics=("parallel", "arbitrary"),
            vmem_limit_bytes=VMEM_LIMIT),
        name="rglru",
    )(xr, gr, conv_w, conv_b, w_gate, b_gate, lam, gn)


_ST_M, _ST_L, _ST_ALPHA, _ST_LFIN = 0, 2, 4, 8


def _fox_body(qi_ref, kj_ref, qt_ref, k_ref, vt_ref, fq_ref, fk_ref, o_ref,
              s0_ref, s1_ref, p0_ref, p1_ref, st_ref, acc_ref, mb_ref,
              *, n_items, tq, tk, dh):
    row_t = lax.broadcasted_iota(jnp.int32, (tk, tq), 0)
    col_t = lax.broadcasted_iota(jnp.int32, (tk, tq), 1)
    mb_ref[0] = jnp.zeros((tk, tq), F32)
    mb_ref[1] = jnp.where(row_t <= col_t, 0.0, NEG_INF)
    st_ref[...] = jnp.ones_like(st_ref)
    acc_ref[...] = jnp.zeros_like(acc_ref)
    p1_ref[...] = jnp.zeros_like(p1_ref)
    head_row = lax.broadcasted_iota(jnp.int32, (2 * dh, 1), 0)

    def stage_a(j, s_ref):
        qt = qt_ref[0, qi_ref[j]]
        kb = k_ref[0, pl.ds(pl.multiple_of(kj_ref[j] * tk, tk), tk), :]
        for h in range(2):
            in_head = (head_row >= h * dh) & (head_row < (h + 1) * dh)
            qm = jnp.where(in_head, qt, jnp.zeros_like(qt))
            s_ref[h] = jnp.dot(kb, qm, preferred_element_type=F32)

    def stage_b(j, s_ref, p_ref, buf):
        qi = qi_ref[j]
        kj = kj_ref[j]
        mb = mb_ref[(kj == qi).astype(jnp.int32)]
        fq2 = fq_ref[0, 0, qi]
        for h in range(2):
            fk = fk_ref[0, 0, pl.ds(pl.multiple_of(kj * tk, tk), tk), h:h + 1]
            fq = fq2[h:h + 1, :]
            z = s_ref[h] - fk + mb
            mz = jnp.max(z, axis=0, keepdims=True) + fq
            m_prev = jnp.where(kj == 0, NEG_INF, st_ref[_ST_M + h:_ST_M + h + 1, :])
            m_new = jnp.maximum(m_prev, mz)
            alpha = jnp.exp2(m_prev - m_new)
            p = jnp.exp2(z - (m_new - fq))
            l_new = (alpha * st_ref[_ST_L + h:_ST_L + h + 1, :]
                     + jnp.sum(p, axis=0, keepdims=True))
            p_ref[h] = p.astype(BF16)
            st_ref[_ST_M + h:_ST_M + h + 1, :] = m_new
            st_ref[_ST_L + h:_ST_L + h + 1, :] = l_new
            ra = _ST_ALPHA + 2 * buf + h
            st_ref[ra:ra + 1, :] = alpha
            rl = _ST_LFIN + 2 * buf + h
            st_ref[rl:rl + 1, :] = l_new

    def stage_c(j, p_ref, buf):
        qi = qi_ref[j]
        vt = vt_ref[0, kj_ref[j]]
        for h in range(2):
            rows = slice(h * dh, (h + 1) * dh)
            pv = jnp.dot(vt[rows, :], p_ref[h], preferred_element_type=F32)
            ra = _ST_ALPHA + 2 * buf + h
            rl = _ST_LFIN + 2 * buf + h
            acc = st_ref[ra:ra + 1, :] * acc_ref[rows, :] + pv
            acc_ref[rows, :] = acc
            o_ref[0, qi, rows, :] = acc / st_ref[rl:rl + 1, :]

    stage_a(0, s0_ref)

    def pair(jj, carry):
        j0 = 2 * jj
        stage_a(j0 + 1, s1_ref)
        stage_b(j0, s0_ref, p0_ref, 0)
        stage_c(jnp.maximum(j0 - 1, 0), p1_ref, 1)
        stage_a(jnp.minimum(j0 + 2, n_items - 1), s0_ref)
        stage_b(j0 + 1, s1_ref, p1_ref, 1)
        stage_c(j0, p0_ref, 0)
        return carry

    lax.fori_loop(0, n_items // 2, pair, 0)
    stage_c(n_items - 1, p1_ref, 1)


def _fox(qt, k, vt, f_row, f_col):
    bsz, nq, width, tq = qt.shape
    seq = k.shape[1]
    dh = width // FOX_HEADS
    pairs = FOX_HEADS // 2
    qi_tab = np.concatenate([np.full(i + 1, i, np.int32) for i in range(nq)])
    kj_tab = np.concatenate([np.arange(i + 1, dtype=np.int32) for i in range(nq)])
    n_items = len(qi_tab)
    assert n_items % 2 == 0, "the pipelined loop handles two items per trip"
    body = functools.partial(_fox_body, n_items=n_items, tq=tq, tk=tq, dh=dh)
    blk_t = pl.BlockSpec((1, nq, 2 * dh, tq), lambda b, p, qi, kj: (b, 0, p, 0))
    grid_spec = pltpu.PrefetchScalarGridSpec(
        num_scalar_prefetch=2,
        grid=(bsz, pairs),
        in_specs=[
            blk_t,
            pl.BlockSpec((1, seq, 2 * dh), lambda b, p, qi, kj: (b, 0, p)),
            blk_t,
            pl.BlockSpec((1, 1, nq, 2, tq), lambda b, p, qi, kj: (b, p, 0, 0, 0)),
            pl.BlockSpec((1, 1, seq, 2), lambda b, p, qi, kj: (b, p, 0, 0)),
        ],
        out_specs=blk_t,
        scratch_shapes=[pltpu.VMEM((2, tq, tq), F32),
                        pltpu.VMEM((2, tq, tq), F32),
                        pltpu.VMEM((2, tq, tq), BF16),
                        pltpu.VMEM((2, tq, tq), BF16),
                        pltpu.VMEM((16, tq), F32),
                        pltpu.VMEM((2 * dh, tq), F32),
                        pltpu.VMEM((2, tq, tq), F32)],
    )
    return pl.pallas_call(
        body,
        grid_spec=grid_spec,
        out_shape=jax.ShapeDtypeStruct((bsz, nq, width, tq), F32),
        compiler_params=pltpu.CompilerParams(
            dimension_semantics=("parallel", "parallel"),
            vmem_limit_bytes=VMEM_LIMIT),
        name="fox",
    )(jnp.asarray(qi_tab), jnp.asarray(kj_tab), qt, k, vt, f_row, f_col)


def _mid_body(x_ref, yl_ref, yf_ref, gnf_ref, wo_ref, g2_ref, wqt_ref, sk_ref,
              x1_ref, h2t_ref, s1_ref, s2_ref, *, half):
    yfn = _rms(yf_ref[0, 0].T, gnf_ref[...]).astype(BF16)
    mix = jnp.dot(yl_ref[...], wo_ref[0:half, :], preferred_element_type=F32)
    mix = mix + jnp.dot(yfn, wo_ref[half:2 * half, :], preferred_element_type=F32)
    x1 = x_ref[...] + mix
    x1_ref[...] = x1
    h2t = _rms(x1, g2_ref[...]).T.astype(BF16)
    h2t_ref[...] = h2t
    qpt = jnp.dot(wqt_ref[...], h2t, preferred_element_type=F32)
    for hp in range(2 * PEER_HEADS):
        qs = qpt[hp * LANES:(hp + 1) * LANES, :].astype(BF16)
        sc = jnp.dot(sk_ref[hp], qs, preferred_element_type=F32)
        if hp % 2 == 0:
            s1_ref[hp // 2] = sc
        else:
            s2_ref[hp // 2] = sc


def _mid(x2d, y_lru, y_fox_t, gn_fox, w_out, g2, wq_t, subkeys, *, tm):
    t, d = x2d.shape
    half = y_lru.shape[1]
    _, n_blk, _, t_blk = y_fox_t.shape
    per_blk = t_blk // tm
    body = functools.partial(_mid_body, half=half)
    tok = lambda i: (i, 0)
    const2 = lambda i: (0, 0)
    sc_shape = jax.ShapeDtypeStruct((PEER_HEADS, PEER_NKEYS, t), F32)
    sc_spec = pl.BlockSpec((PEER_HEADS, PEER_NKEYS, tm), lambda i: (0, 0, i))
    return pl.pallas_call(
        body,
        grid=(t // tm,),
        in_specs=[
            pl.BlockSpec((tm, d), tok),
            pl.BlockSpec((tm, half), tok),
            pl.BlockSpec((1, 1, half, tm),
                         lambda i: (i // (per_blk * n_blk), (i // per_blk) % n_blk, 0,
                                    i % per_blk)),
            pl.BlockSpec((1, half), const2),
            pl.BlockSpec(w_out.shape, const2),
            pl.BlockSpec((1, d), const2),
            pl.BlockSpec(wq_t.shape, const2),
            pl.BlockSpec(subkeys.shape, lambda i: (0, 0, 0)),
        ],
        out_specs=[pl.BlockSpec((tm, d), tok),
                   pl.BlockSpec((d, tm), lambda i: (0, i)),
                   sc_spec, sc_spec],
        out_shape=[jax.ShapeDtypeStruct((t, d), F32),
                   jax.ShapeDtypeStruct((d, t), BF16),
                   sc_shape, sc_shape],
        compiler_params=pltpu.CompilerParams(
            dimension_semantics=("parallel",),
            vmem_limit_bytes=VMEM_LIMIT),
        name="mid",
    )(x2d, y_lru, y_fox_t, gn_fox, w_out, g2, wq_t, subkeys)


_CAND = [(a, b) for a in range(PEER_TOPK) for b in range(PEER_TOPK)
         if (a + 1) * (b + 1) <= PEER_TOPK]
_CAND_ROWS = -(-len(_CAND) // SUBLANES) * SUBLANES


def _bf16_pair_word(x):
    bits = pltpu.bitcast(x.astype(BF16).astype(F32), jnp.uint32)
    return bits | (bits >> 16)


def _top_values(x, n, want_rank=False):
    vals = []
    rank = jnp.full(x.shape, float(n), F32) if want_rank else None
    for r in range(n):
        mx = jnp.max(x, axis=0, keepdims=True)
        vals.append(mx)
        hit = x == mx
        x = jnp.where(hit, NEG_INF, x)
        if want_rank:
            rank = jnp.where(hit, float(r), rank)
    return vals, rank


def _gates_body(s1_ref, s2_ref, n_ref, p1_ref, rank_ref, p2_ref, cand_ref):
    s1 = s1_ref[0]
    s2 = s2_ref[0]
    a, _ = _top_values(s1, PEER_TOPK)
    b, rank = _top_values(s2, PEER_TOPK, want_rank=True)
    cand_ref[...] = jnp.full_like(cand_ref, NEG_INF)
    sums = [a[ia] + b[ib] for ia, ib in _CAND]
    for r, v in enumerate(sums):
        cand_ref[r:r + 1, :] = v
    c, _ = _top_values(cand_ref[...], PEER_TOPK)
    z = jnp.zeros_like(c[0])
    for r in range(PEER_TOPK):
        z = z + jnp.exp(c[r] - c[0])
    tau = c[PEER_TOPK - 1]
    picked = [jnp.zeros_like(tau) for _ in range(PEER_TOPK)]
    for (ia, _), v in zip(_CAND, sums):
        picked[ia] = picked[ia] + jnp.where(v >= tau, 1.0, 0.0)
    n = jnp.zeros_like(s1)
    for ia in range(PEER_TOPK):
        n = jnp.where(s1 == a[ia], picked[ia], n)
    n_ref[0] = _bf16_pair_word(n)
    p1_ref[0] = _bf16_pair_word(0.5 * jnp.exp(s1 - a[0]) / z)
    rank_ref[0] = rank.astype(BF16)
    p2_ref[0] = jnp.exp(s2 - b[0]).astype(BF16)


def _gates(s1, s2, *, tm):
    heads, nk, t = s1.shape
    spec = pl.BlockSpec((1, nk, tm), lambda i, h: (h, 0, i))
    shape = jax.ShapeDtypeStruct(s1.shape, jnp.uint32)
    shape_b = jax.ShapeDtypeStruct(s1.shape, BF16)
    return pl.pallas_call(
        _gates_body,
        grid=(t // tm, heads),
        in_specs=[spec, spec],
        out_specs=[spec, spec, spec, spec],
        out_shape=[shape, shape, shape_b, shape_b],
        scratch_shapes=[pltpu.VMEM((_CAND_ROWS, tm), F32)],
        compiler_params=pltpu.CompilerParams(
            dimension_semantics=("parallel", "parallel"),
            vmem_limit_bytes=VMEM_LIMIT),
        name="gates",
    )(s1, s2)


def _peer_body(h2_ref, u_ref, vt_ref, rank_ref, p2_ref, n_ref, p1_ref, x1_ref, gf_ref,
               o_ref, pre_ref, acc_ref, *, per_tile, rows_per_item):
    s = pl.program_id(0)
    n = s - 1

    @pl.when(s == 0)
    def _():
        pre_ref[...] = jnp.zeros_like(pre_ref)
        acc_ref[...] = jnp.zeros_like(acc_ref)

    pre_next = jnp.dot(u_ref[...], h2_ref[...], preferred_element_type=F32)

    tm = pre_ref.shape[1]

    def packed_row(ref, h, r):
        words = jnp.broadcast_to(ref[h, r:r + 1, :], (SUBLANES, tm))
        return pltpu.bitcast(words, BF16)[None]

    chunks = []
    groups = PEER_NKEYS // BF16_ROWS
    for r in range(rows_per_item):
        w = jnp.zeros((groups, BF16_ROWS, tm), BF16)
        for h in range(PEER_HEADS):
            rank = rank_ref[h].reshape(groups, BF16_ROWS, tm)
            p2 = p2_ref[h].reshape(groups, BF16_ROWS, tm)
            sel = jnp.where(rank < packed_row(n_ref, h, r), p2, jnp.zeros_like(p2))
            w = w + sel * packed_row(p1_ref, h, r)
        rows = slice(r * PEER_NKEYS, (r + 1) * PEER_NKEYS)
        act = _gelu_tanh_twice(pre_ref[rows, :]) * w.reshape(PEER_NKEYS, tm).astype(F32)
        chunks.append(act.astype(BF16))
    wact = jnp.concatenate(chunks, axis=0)

    pv = jnp.dot(vt_ref[...], wact, preferred_element_type=F32)
    first = (n % per_tile) == 0
    acc_ref[...] = jnp.where(first, pv, acc_ref[...] + pv)
    pre_ref[...] = pre_next

    @pl.when((n >= 0) & (n % per_tile == per_tile - 1))
    def _():
        xo = x1_ref[...] + acc_ref[...].T
        o_ref[...] = _rms(xo, gf_ref[...])


def _peer(h2t, u, vt, rank, p2, n_word, p1_word, x1, gf, *, tm, te):
    d, t = h2t.shape
    n_exp = u.shape[0]
    per_tile = n_exp // te
    rows_per_item = te // PEER_NKEYS
    n_items = (t // tm) * per_tile
    body = functools.partial(_peer_body, per_tile=per_tile, rows_per_item=rows_per_item)

    def item(s, off):
        return jnp.clip(s - 1 + off, 0, n_items - 1)

    tile = lambda s, off: item(s, off) // per_tile
    blk = lambda s, off: item(s, off) % per_tile
    full_spec = pl.BlockSpec((PEER_HEADS, PEER_NKEYS, tm), lambda s: (0, 0, tile(s, 0)))
    rows_spec = pl.BlockSpec((PEER_HEADS, rows_per_item, tm),
                             lambda s: (0, blk(s, 0), tile(s, 0)))
    tok_spec = pl.BlockSpec((tm, d), lambda s: (tile(s, 0), 0))
    return pl.pallas_call(
        body,
        grid=(n_items + 1,),
        in_specs=[
            pl.BlockSpec((d, tm), lambda s: (0, tile(s, 1))),
            pl.BlockSpec((te, d), lambda s: (blk(s, 1), 0)),
            pl.BlockSpec((d, te), lambda s: (0, blk(s, 0))),
            full_spec, full_spec, rows_spec, rows_spec, tok_spec,
            pl.BlockSpec((1, d), lambda s: (0, 0)),
        ],
        out_specs=tok_spec,
        out_shape=jax.ShapeDtypeStruct((t, d), F32),
        scratch_shapes=[pltpu.VMEM((te, tm), F32), pltpu.VMEM((d, tm), F32)],
        compiler_params=pltpu.CompilerParams(
            dimension_semantics=("arbitrary",),
            vmem_limit_bytes=VMEM_LIMIT),
        name="peer",
    )(h2t, u, vt, rank, p2, n_word, p1_word, x1, gf)


def _block_diag(w):
    heads, dh, _ = w.shape
    eye = jnp.eye(heads, dtype=w.dtype)
    return jnp.einsum("hij,hg->higj", w, eye).reshape(heads * dh, heads * dh)


def _layer(x, norm1_g, w_in, conv_w, conv_b, lru_wa, lru_ba, lru_wx, lru_bx, lru_lambda,
           fox_bf, gn_lru_g, gn_fox_g, w_out, norm2_g, peer_wq, peer_subkeys, peer_u,
           peer_v, out_g):
    bsz, seq, d = x.shape
    width = lru_lambda.shape[0]
    row = lambda v: v.reshape(1, -1).astype(F32)

    w_main = w_in[:, :5 * width].astype(BF16)
    w_fl = jnp.pad(w_in[:, 5 * width:], ((0, 0), (0, LANES - FOX_HEADS))).astype(BF16)
    b_fl = jnp.pad(fox_bf.astype(F32), (0, LANES - FOX_HEADS)).reshape(1, LANES)
    t_blk = FOX_BLOCK
    xr, gr, qt, k, vt, f_t = _inproj(x, row(norm1_g), w_main, w_fl, b_fl, tm=t_blk)

    w_gate = jnp.concatenate([_block_diag(lru_wa), _block_diag(lru_wx)], axis=1).astype(BF16)
    b_gate = jnp.concatenate([lru_ba, lru_bx]).reshape(1, -1).astype(F32)
    y_lru = _rglru(xr, gr, conv_w.astype(F32), row(conv_b), w_gate, b_gate,
                   row(lru_lambda), row(gn_lru_g), ts=256)

    pairs = FOX_HEADS // 2
    f_pair = f_t.reshape(bsz, pairs, 2, seq)
    f_col = f_pair.transpose(0, 1, 3, 2)
    f_row = f_pair.reshape(bsz, pairs, 2, seq // t_blk, t_blk).transpose(0, 1, 3, 2, 4)
    y_fox_t = _fox(qt, k, vt, f_row, f_col)

    t = bsz * seq
    n_hp = 2 * PEER_HEADS
    subkeys = peer_subkeys.reshape(n_hp, PEER_NKEYS, -1).astype(BF16)
    x1, h2t, s1, s2 = _mid(x.reshape(t, d), y_lru.reshape(t, width), y_fox_t,
                           row(gn_fox_g), w_out.astype(BF16), row(norm2_g),
                           peer_wq.T.astype(BF16), subkeys, tm=256)

    n_sel, p1, rank2, p2 = _gates(s1, s2, tm=256)
    out = _peer(h2t, peer_u.astype(BF16), peer_v.T.astype(BF16), rank2, p2, n_sel, p1, x1,
                row(out_g), tm=512, te=2048)
    return out.reshape(bsz, seq, d)


def kernel(x, norm1_g, w_in, conv_w, conv_b, lru_wa, lru_ba, lru_wx, lru_bx, lru_lambda,
           fox_bf, gn_lru_g, gn_fox_g, w_out, norm2_g, peer_wq, peer_subkeys, peer_u,
           peer_v, final_g):
    depth = w_in.shape[0]
    assert depth == 1, "the fused final RMSNorm assumes a single trunk layer"
    return _layer(x, norm1_g[0], w_in[0], conv_w[0], conv_b[0], lru_wa[0], lru_ba[0],
                  lru_wx[0], lru_bx[0], lru_lambda[0], fox_bf[0], gn_lru_g[0], gn_fox_g[0],
                  w_out[0], norm2_g[0], peer_wq[0], peer_subkeys[0], peer_u[0], peer_v[0],
                  final_g)
```

```python
import functools

import jax
import jax.numpy as jnp
import numpy as np
from jax import lax
from jax.experimental import pallas as pl
from jax.experimental.pallas import tpu as pltpu

F32 = jnp.float32
BF16 = jnp.bfloat16

RMS_EPS = 1e-6
LRU_C = 8.0
CONV_WIDTH = 4
LRU_HEADS = 8
FOX_HEADS = 8
PEER_HEADS = 8
PEER_NKEYS = 128
PEER_TOPK = 16
FOX_BLOCK = 512

LANES = 128
SUBLANES = 8
VMEM_LIMIT = 56 * 1024 * 1024

NEG_INF = float("-inf")
LOG2E = 1.4426950408889634
_GELU_K = 0.7978845608028654
BF16_ROWS = 16


def _rms(x, g):
    return x * lax.rsqrt(jnp.mean(x * x, axis=-1, keepdims=True) + RMS_EPS) * g


def _gelu_tanh_twice(x):
    inner = x * (_GELU_K + (_GELU_K * 0.044715) * (x * x))
    return x + x * jnp.tanh(inner)


def _inproj_body(x_ref, g_ref, w_ref, wfl_ref, bf_ref,
                 xr_ref, gr_ref, qt_ref, k_ref, vt_ref, ft_ref, carry_ref,
                 *, tm, width, q_scale):
    s = pl.program_id(1)

    @pl.when(s == 0)
    def _():
        carry_ref[...] = jnp.zeros_like(carry_ref)

    hb = _rms(x_ref[0], g_ref[...]).astype(BF16)

    def piece(i):
        return jnp.dot(hb, w_ref[:, i * width:(i + 1) * width],
                       preferred_element_type=F32)

    xr_ref[0] = piece(0)
    gr_ref[0] = piece(1)
    qt_ref[0, 0] = (piece(2) * (q_scale * LOG2E)).T.astype(BF16)
    k_ref[0] = piece(3).astype(BF16)
    vt_ref[0, 0] = piece(4).T.astype(BF16)

    fl = jnp.dot(hb, wfl_ref[...], preferred_element_type=F32)
    c = jax.nn.log_sigmoid(fl + bf_ref[...]) * LOG2E
    row = lax.broadcasted_iota(jnp.int32, c.shape, 0)
    sh = 1
    while sh < tm:
        c = c + jnp.where(row >= sh, pltpu.roll(c, sh, axis=0), 0.0)
        sh *= 2
    f = c + carry_ref[...]
    carry_ref[...] = f[tm - 1:tm, :]
    ft_ref[0] = f.T[0:FOX_HEADS, :]


def _inproj(x, g, w_main, w_fl, b_fl, *, tm):
    bsz, seq, d = x.shape
    width = w_main.shape[1] // 5
    body = functools.partial(_inproj_body, tm=tm, width=width,
                             q_scale=(width // FOX_HEADS) ** -0.5)
    tok = lambda b, s: (b, s, 0)
    const = lambda b, s: (0, 0)
    big = jax.ShapeDtypeStruct((bsz, seq, width), F32)
    bigb = jax.ShapeDtypeStruct((bsz, seq, width), BF16)
    blocked_t = jax.ShapeDtypeStruct((bsz, seq // tm, width, tm), BF16)
    return pl.pallas_call(
        body,
        grid=(bsz, seq // tm),
        in_specs=[
            pl.BlockSpec((1, tm, d), tok),
            pl.BlockSpec((1, d), const),
            pl.BlockSpec(w_main.shape, const),
            pl.BlockSpec(w_fl.shape, const),
            pl.BlockSpec((1, LANES), const),
        ],
        out_specs=[pl.BlockSpec((1, tm, width), tok),
                   pl.BlockSpec((1, tm, width), tok),
                   pl.BlockSpec((1, 1, width, tm), lambda b, s: (b, s, 0, 0)),
                   pl.BlockSpec((1, tm, width), tok),
                   pl.BlockSpec((1, 1, width, tm), lambda b, s: (b, s, 0, 0)),
                   pl.BlockSpec((1, FOX_HEADS, tm), lambda b, s: (b, 0, s))],
        out_shape=[big, big, blocked_t, bigb, blocked_t,
                   jax.ShapeDtypeStruct((bsz, FOX_HEADS, seq), F32)],
        scratch_shapes=[pltpu.VMEM((1, LANES), F32)],
        compiler_params=pltpu.CompilerParams(
            dimension_semantics=("parallel", "arbitrary"),
            vmem_limit_bytes=VMEM_LIMIT),
        name="inproj",
    )(x, g, w_main, w_fl, b_fl)


def _rglru_body(xr_ref, gr_ref, cw_ref, cb_ref, wg_ref, bg_ref, lam_ref, gn_ref,
                y_ref, xbuf_ref, hc_ref, *, ts, width):
    s = pl.program_id(1)

    @pl.when(s == 0)
    def _():
        xbuf_ref[0:SUBLANES, :] = jnp.zeros((SUBLANES, width), F32)
        hc_ref[...] = jnp.zeros_like(hc_ref)

    x = xr_ref[0]
    xbuf_ref[SUBLANES:SUBLANES + ts, :] = x
    cw = cw_ref[...]
    conv = cb_ref[...] + cw[CONV_WIDTH - 1:CONV_WIDTH, :] * x
    for back in range(1, CONV_WIDTH):
        k = CONV_WIDTH - 1 - back
        conv = conv + cw[k:k + 1, :] * xbuf_ref[SUBLANES - back:SUBLANES - back + ts, :]
    xbuf_ref[0:SUBLANES, :] = x[ts - SUBLANES:ts, :]

    gates = jnp.dot(conv.astype(BF16), wg_ref[...],
                    preferred_element_type=F32) + bg_ref[...]
    r = jax.nn.sigmoid(gates[:, :width])
    i = jax.nn.sigmoid(gates[:, width:])
    log_a = (-LRU_C) * r * jax.nn.softplus(-lam_ref[...])
    a = jnp.exp(log_a)
    u = jnp.sqrt(1.0 - a * a) * (i * conv)

    row = lax.broadcasted_iota(jnp.int32, a.shape, 0)
    sh = 1
    while sh < ts:
        keep = row >= sh
        a_s = jnp.where(keep, pltpu.roll(a, sh, axis=0), 1.0)
        u_s = jnp.where(keep, pltpu.roll(u, sh, axis=0), 0.0)
        u = a * u_s + u
        a = a * a_s
        sh *= 2
    h = u + a * hc_ref[...]
    hc_ref[...] = h[ts - 1:ts, :]

    y = h * jax.nn.gelu(gr_ref[0])
    y_ref[0] = _rms(y, gn_ref[...]).astype(BF16)


def _rglru(xr, gr, conv_w, conv_b, w_gate, b_gate, lam, gn, *, ts):
    bsz, seq, width = xr.shape
    body = functools.partial(_rglru_body, ts=ts, width=width)
    tok = lambda b, s: (b, s, 0)
    const = lambda b, s: (0, 0)
    return pl.pallas_call(
        body,
        grid=(bsz, seq // ts),
        in_specs=[
            pl.BlockSpec((1, ts, width), tok),
            pl.BlockSpec((1, ts, width), tok),
            pl.BlockSpec(conv_w.shape, const),
            pl.BlockSpec((1, width), const),
            pl.BlockSpec(w_gate.shape, const),
            pl.BlockSpec((1, 2 * width), const),
            pl.BlockSpec((1, width), const),
            pl.BlockSpec((1, width), const),
        ],
        out_specs=pl.BlockSpec((1, ts, width), tok),
        out_shape=jax.ShapeDtypeStruct((bsz, seq, width), BF16),
        scratch_shapes=[pltpu.VMEM((ts + SUBLANES, width), F32),
                        pltpu.VMEM((1, width), F32)],
        compiler_params=pltpu.CompilerParams(
            dimension_semantics=("parallel", "arbitrary"),
            vmem_limit_bytes=VMEM_LIMIT),
        name="rglru",
    )(xr, gr, conv_w, conv_b, w_gate, b_gate, lam, gn)


_ST_M, _ST_L = 0, 2
_PB_ALPHA, _PB_LFIN = 0, 4


def _fox_body(qi_off_ref, kj_off_ref, qi_diag_ref, qt_ref, k_ref, vt_ref, fq_ref, fk_ref,
              o_ref, s0_ref, s1_ref, p0_ref, p1_ref, p2_ref, p3_ref, st_ref, pbo_ref, pbd_ref,
              acc_ref, mb_ref,
              *, n_off, n_diag, tq, tk, dh):
    row_t = lax.broadcasted_iota(jnp.int32, (tk, tq), 0)
    col_t = lax.broadcasted_iota(jnp.int32, (tk, tq), 1)
    mb_ref[...] = jnp.where(row_t <= col_t, 0.0, NEG_INF)
    st_ref[:, _ST_M:_ST_M + 2, :] = jnp.full((st_ref.shape[0], 2, tq), NEG_INF, F32)
    st_ref[:, _ST_L:_ST_L + 2, :] = jnp.zeros((st_ref.shape[0], 2, tq), F32)
    acc_ref[...] = jnp.zeros_like(acc_ref)
    head_row = lax.broadcasted_iota(jnp.int32, (2 * dh, 1), 0)

    def stage_a(qi, kj, s_ref):
        qt = qt_ref[0, qi]
        kb = k_ref[0, pl.ds(pl.multiple_of(kj * tk, tk), tk), :]
        for h in range(2):
            in_head = (head_row >= h * dh) & (head_row < (h + 1) * dh)
            qm = jnp.where(in_head, qt, jnp.zeros_like(qt))
            s_ref[h] = jnp.dot(kb, qm, preferred_element_type=F32)

    def stage_b(qi, kj, s_ref, p_ref, pb_ref, buf, diagonal):
        fq2 = fq_ref[0, 0, qi]
        for h in range(2):
            fk = fk_ref[0, 0, pl.ds(pl.multiple_of(kj * tk, tk), tk), h:h + 1]
            fq = fq2[h:h + 1, :]
            z = s_ref[h] - fk
            if diagonal:
                z = z + mb_ref[...]
            mz = jnp.max(z, axis=0, keepdims=True) + fq
            m_prev = st_ref[qi, _ST_M + h:_ST_M + h + 1, :]
            m_new = jnp.maximum(m_prev, mz)
            alpha = jnp.exp2(m_prev - m_new)
            p = jnp.exp2(z - (m_new - fq))
            l_new = (alpha * st_ref[qi, _ST_L + h:_ST_L + h + 1, :]
                     + jnp.sum(p, axis=0, keepdims=True))
            p_ref[h] = p.astype(BF16)
            st_ref[qi, _ST_M + h:_ST_M + h + 1, :] = m_new
            st_ref[qi, _ST_L + h:_ST_L + h + 1, :] = l_new
            ra = _PB_ALPHA + 2 * buf + h
            pb_ref[ra:ra + 1, :] = alpha
            rl = _PB_LFIN + 2 * buf + h
            pb_ref[rl:rl + 1, :] = l_new

    def stage_c(qi, kj, p_ref, pb_ref, buf, diagonal):
        vt = vt_ref[0, kj]
        for h in range(2):
            rows = slice(h * dh, (h + 1) * dh)
            pv = jnp.dot(vt[rows, :], p_ref[h], preferred_element_type=F32)
            ra = _PB_ALPHA + 2 * buf + h
            acc = pb_ref[ra:ra + 1, :] * acc_ref[qi, rows, :] + pv
            acc_ref[qi, rows, :] = acc
            if diagonal:
                rl = _PB_LFIN + 2 * buf + h
                o_ref[0, qi, rows, :] = acc / pb_ref[rl:rl + 1, :]

    def run(n_items, item, diagonal, pa_ref, pz_ref, pb_ref):
        stage_a(*item(0), s0_ref)

        def pair(jj, carry):
            j0 = 2 * jj
            stage_a(*item(j0 + 1), s1_ref)
            stage_b(*item(j0), s0_ref, pa_ref, pb_ref, 0, diagonal)
            stage_c(*item(jnp.maximum(j0 - 1, 0)), pz_ref, pb_ref, 1, diagonal)
            stage_a(*item(jnp.minimum(j0 + 2, n_items - 1)), s0_ref)
            stage_b(*item(j0 + 1), s1_ref, pz_ref, pb_ref, 1, diagonal)
            stage_c(*item(j0), pa_ref, pb_ref, 0, diagonal)
            return carry

        lax.fori_loop(0, n_items // 2, pair, 0)
        stage_c(*item(n_items - 1), pz_ref, pb_ref, 1, diagonal)

    p1_ref[...] = jnp.zeros_like(p1_ref)
    p3_ref[...] = jnp.zeros_like(p3_ref)
    pbo_ref[...] = jnp.ones_like(pbo_ref)
    pbd_ref[...] = jnp.ones_like(pbd_ref)
    run(n_off, lambda j: (qi_off_ref[j], kj_off_ref[j]), False, p0_ref, p1_ref, pbo_ref)
    run(n_diag, lambda j: (qi_diag_ref[j], qi_diag_ref[j]), True, p2_ref, p3_ref, pbd_ref)


def _fox(qt, k, vt, f_row, f_col):
    bsz, nq, width, tq = qt.shape
    seq = k.shape[1]
    dh = width // FOX_HEADS
    pairs = FOX_HEADS // 2
    qi_off = np.concatenate([np.full(i, i, np.int32) for i in range(1, nq)])
    kj_off = np.concatenate([np.arange(i, dtype=np.int32) for i in range(1, nq)])
    qi_diag = np.arange(nq, dtype=np.int32)
    assert len(qi_off) % 2 == 0 and nq % 2 == 0, "the pipelined loops take two items per trip"
    body = functools.partial(_fox_body, n_off=len(qi_off), n_diag=nq, tq=tq, tk=tq, dh=dh)
    blk_t = pl.BlockSpec((1, nq, 2 * dh, tq), lambda b, p, *_: (b, 0, p, 0))
    grid_spec = pltpu.PrefetchScalarGridSpec(
        num_scalar_prefetch=3,
        grid=(bsz, pairs),
        in_specs=[
            blk_t,
            pl.BlockSpec((1, seq, 2 * dh), lambda b, p, *_: (b, 0, p)),
            blk_t,
            pl.BlockSpec((1, 1, nq, 2, tq), lambda b, p, *_: (b, p, 0, 0, 0)),
            pl.BlockSpec((1, 1, seq, 2), lambda b, p, *_: (b, p, 0, 0)),
        ],
        out_specs=blk_t,
        scratch_shapes=[pltpu.VMEM((2, tq, tq), F32),
                        pltpu.VMEM((2, tq, tq), F32),
                        pltpu.VMEM((2, tq, tq), BF16),
                        pltpu.VMEM((2, tq, tq), BF16),
                        pltpu.VMEM((2, tq, tq), BF16),
                        pltpu.VMEM((2, tq, tq), BF16),
                        pltpu.VMEM((nq, SUBLANES, tq), F32),
                        pltpu.VMEM((SUBLANES, tq), F32),
                        pltpu.VMEM((SUBLANES, tq), F32),
                        pltpu.VMEM((nq, 2 * dh, tq), F32),
                        pltpu.VMEM((tq, tq), F32)],
    )
    return pl.pallas_call(
        body,
        grid_spec=grid_spec,
        out_shape=jax.ShapeDtypeStruct((bsz, nq, width, tq), F32),
        compiler_params=pltpu.CompilerParams(
            dimension_semantics=("parallel", "parallel"),
            vmem_limit_bytes=VMEM_LIMIT),
        name="fox",
    )(jnp.asarray(qi_off), jnp.asarray(kj_off), jnp.asarray(qi_diag), qt, k, vt, f_row, f_col)


def _mid_body(x_ref, yl_ref, yf_ref, gnf_ref, wo_ref, g2_ref, wqt_ref, sk_ref,
              x1_ref, h2t_ref, s1_ref, s2_ref, *, half):
    yfn = _rms(yf_ref[0, 0].T, gnf_ref[...]).astype(BF16)
    mix = jnp.dot(yl_ref[...], wo_ref[0:half, :], preferred_element_type=F32)
    mix = mix + jnp.dot(yfn, wo_ref[half:2 * half, :], preferred_element_type=F32)
    x1 = x_ref[...] + mix
    x1_ref[...] = x1
    h2t = _rms(x1, g2_ref[...]).T.astype(BF16)
    h2t_ref[...] = h2t
    qpt = jnp.dot(wqt_ref[...], h2t, preferred_element_type=F32)
    for hp in range(2 * PEER_HEADS):
        qs = qpt[hp * LANES:(hp + 1) * LANES, :].astype(BF16)
        sc = jnp.dot(sk_ref[hp], qs, preferred_element_type=F32)
        if hp % 2 == 0:
            s1_ref[hp // 2] = sc
        else:
            s2_ref[hp // 2] = sc


def _mid(x2d, y_lru, y_fox_t, gn_fox, w_out, g2, wq_t, subkeys, *, tm):
    t, d = x2d.shape
    half = y_lru.shape[1]
    _, n_blk, _, t_blk = y_fox_t.shape
    per_blk = t_blk // tm
    body = functools.partial(_mid_body, half=half)
    tok = lambda i: (i, 0)
    const2 = lambda i: (0, 0)
    sc_shape = jax.ShapeDtypeStruct((PEER_HEADS, PEER_NKEYS, t), F32)
    sc_spec = pl.BlockSpec((PEER_HEADS, PEER_NKEYS, tm), lambda i: (0, 0, i))
    return pl.pallas_call(
        body,
        grid=(t // tm,),
        in_specs=[
            pl.BlockSpec((tm, d), tok),
            pl.BlockSpec((tm, half), tok),
            pl.BlockSpec((1, 1, half, tm),
                         lambda i: (i // (per_blk * n_blk), (i // per_blk) % n_blk, 0,
                                    i % per_blk)),
            pl.BlockSpec((1, half), const2),
            pl.BlockSpec(w_out.shape, const2),
            pl.BlockSpec((1, d), const2),
            pl.BlockSpec(wq_t.shape, const2),
            pl.BlockSpec(subkeys.shape, lambda i: (0, 0, 0)),
        ],
        out_specs=[pl.BlockSpec((tm, d), tok),
                   pl.BlockSpec((d, tm), lambda i: (0, i)),
                   sc_spec, sc_spec],
        out_shape=[jax.ShapeDtypeStruct((t, d), F32),
                   jax.ShapeDtypeStruct((d, t), BF16),
                   sc_shape, sc_shape],
        compiler_params=pltpu.CompilerParams(
            dimension_semantics=("parallel",),
            vmem_limit_bytes=VMEM_LIMIT),
        name="mid",
    )(x2d, y_lru, y_fox_t, gn_fox, w_out, g2, wq_t, subkeys)


_CAND = [(a, b) for a in range(PEER_TOPK) for b in range(PEER_TOPK)
         if (a + 1) * (b + 1) <= PEER_TOPK]
_CAND_ROWS = -(-len(_CAND) // SUBLANES) * SUBLANES


def _bf16_pair_word(x):
    bits = pltpu.bitcast(x.astype(BF16).astype(F32), jnp.uint32)
    return bits | (bits >> 16)


def _top_values(x, n, want_rank=False):
    vals = []
    rank = jnp.full(x.shape, float(n), F32) if want_rank else None
    for r in range(n):
        mx = jnp.max(x, axis=0, keepdims=True)
        vals.append(mx)
        hit = x == mx
        x = jnp.where(hit, NEG_INF, x)
        if want_rank:
            rank = jnp.where(hit, float(r), rank)
    return vals, rank


def _gates_body(s1_ref, s2_ref, n_ref, p1_ref, rank_ref, p2_ref, cand_ref):
    s1 = s1_ref[0]
    s2 = s2_ref[0]
    a, _ = _top_values(s1, PEER_TOPK)
    b, rank = _top_values(s2, PEER_TOPK, want_rank=True)
    cand_ref[...] = jnp.full_like(cand_ref, NEG_INF)
    sums = [a[ia] + b[ib] for ia, ib in _CAND]
    for r, v in enumerate(sums):
        cand_ref[r:r + 1, :] = v
    c, _ = _top_values(cand_ref[...], PEER_TOPK)
    z = jnp.zeros_like(c[0])
    for r in range(PEER_TOPK):
        z = z + jnp.exp(c[r] - c[0])
    tau = c[PEER_TOPK - 1]
    picked = [jnp.zeros_like(tau) for _ in range(PEER_TOPK)]
    for (ia, _), v in zip(_CAND, sums):
        picked[ia] = picked[ia] + jnp.where(v >= tau, 1.0, 0.0)
    n = jnp.zeros_like(s1)
    for ia in range(PEER_TOPK):
        n = jnp.where(s1 == a[ia], picked[ia], n)
    n_ref[0] = _bf16_pair_word(n)
    p1_ref[0] = _bf16_pair_word(0.5 * jnp.exp(s1 - a[0]) / z)
    rank_ref[0] = rank.astype(BF16)
    p2_ref[0] = jnp.exp(s2 - b[0]).astype(BF16)


def _gates(s1, s2, *, tm):
    heads, nk, t = s1.shape
    spec = pl.BlockSpec((1, nk, tm), lambda i, h: (h, 0, i))
    shape = jax.ShapeDtypeStruct(s1.shape, jnp.uint32)
    shape_b = jax.ShapeDtypeStruct(s1.shape, BF16)
    return pl.pallas_call(
        _gates_body,
        grid=(t // tm, heads),
        in_specs=[spec, spec],
        out_specs=[spec, spec, spec, spec],
        out_shape=[shape, shape, shape_b, shape_b],
        scratch_shapes=[pltpu.VMEM((_CAND_ROWS, tm), F32)],
        compiler_params=pltpu.CompilerParams(
            dimension_semantics=("parallel", "parallel"),
            vmem_limit_bytes=VMEM_LIMIT),
        name="gates",
    )(s1, s2)


def _peer_body(h2_ref, u_ref, vt_ref, rank_ref, p2_ref, n_ref, p1_ref, x1_ref, gf_ref,
               o_ref, pre_ref, acc_ref, *, per_tile, rows_per_item):
    s = pl.program_id(0)
    n = s - 1

    @pl.when(s == 0)
    def _():
        pre_ref[...] = jnp.zeros_like(pre_ref)
        acc_ref[...] = jnp.zeros_like(acc_ref)

    pre_next = jnp.dot(u_ref[...], h2_ref[...], preferred_element_type=F32)

    tm = pre_ref.shape[1]

    def packed_row(ref, h, r):
        words = jnp.broadcast_to(ref[h, r:r + 1, :], (SUBLANES, tm))
        return pltpu.bitcast(words, BF16)[None]

    chunks = []
    groups = PEER_NKEYS // BF16_ROWS
    for r in range(rows_per_item):
        w = jnp.zeros((groups, BF16_ROWS, tm), BF16)
        for h in range(PEER_HEADS):
            rank = rank_ref[h].reshape(groups, BF16_ROWS, tm)
            p2 = p2_ref[h].reshape(groups, BF16_ROWS, tm)
            sel = jnp.where(rank < packed_row(n_ref, h, r), p2, jnp.zeros_like(p2))
            w = w + sel * packed_row(p1_ref, h, r)
        rows = slice(r * PEER_NKEYS, (r + 1) * PEER_NKEYS)
        act = _gelu_tanh_twice(pre_ref[rows, :]) * w.reshape(PEER_NKEYS, tm).astype(F32)
        chunks.append(act.astype(BF16))
    wact = jnp.concatenate(chunks, axis=0)

    pv = jnp.dot(vt_ref[...], wact, preferred_element_type=F32)
    first = (n % per_tile) == 0
    acc_ref[...] = jnp.where(first, pv, acc_ref[...] + pv)
    pre_ref[...] = pre_next

    @pl.when((n >= 0) & (n % per_tile == per_tile - 1))
    def _():
        xo = x1_ref[...] + acc_ref[...].T
        o_ref[...] = _rms(xo, gf_ref[...])


def _peer(h2t, u, vt, rank, p2, n_word, p1_word, x1, gf, *, tm, te):
    d, t = h2t.shape
    n_exp = u.shape[0]
    per_tile = n_exp // te
    rows_per_item = te // PEER_NKEYS
    n_items = (t // tm) * per_tile
    body = functools.partial(_peer_body, per_tile=per_tile, rows_per_item=rows_per_item)

    def item(s, off):
        return jnp.clip(s - 1 + off, 0, n_items - 1)

    tile = lambda s, off: item(s, off) // per_tile
    blk = lambda s, off: item(s, off) % per_tile
    full_spec = pl.BlockSpec((PEER_HEADS, PEER_NKEYS, tm), lambda s: (0, 0, tile(s, 0)))
    rows_spec = pl.BlockSpec((PEER_HEADS, rows_per_item, tm),
                             lambda s: (0, blk(s, 0), tile(s, 0)))
    tok_spec = pl.BlockSpec((tm, d), lambda s: (tile(s, 0), 0))
    return pl.pallas_call(
        body,
        grid=(n_items + 1,),
        in_specs=[
            pl.BlockSpec((d, tm), lambda s: (0, tile(s, 1))),
            pl.BlockSpec((te, d), lambda s: (blk(s, 1), 0)),
            pl.BlockSpec((d, te), lambda s: (0, blk(s, 0))),
            full_spec, full_spec, rows_spec, rows_spec, tok_spec,
            pl.BlockSpec((1, d), lambda s: (0, 0)),
        ],
        out_specs=tok_spec,
        out_shape=jax.ShapeDtypeStruct((t, d), F32),
        scratch_shapes=[pltpu.VMEM((te, tm), F32), pltpu.VMEM((d, tm), F32)],
        compiler_params=pltpu.CompilerParams(
            dimension_semantics=("arbitrary",),
            vmem_limit_bytes=VMEM_LIMIT),
        name="peer",
    )(h2t, u, vt, rank, p2, n_word, p1_word, x1, gf)


def _block_diag(w):
    heads, dh, _ = w.shape
    eye = jnp.eye(heads, dtype=w.dtype)
    return jnp.einsum("hij,hg->higj", w, eye).reshape(heads * dh, heads * dh)


def _layer(x, norm1_g, w_in, conv_w, conv_b, lru_wa, lru_ba, lru_wx, lru_bx, lru_lambda,
           fox_bf, gn_lru_g, gn_fox_g, w_out, norm2_g, peer_wq, peer_subkeys, peer_u,
           peer_v, out_g):
    bsz, seq, d = x.shape
    width = lru_lambda.shape[0]
    row = lambda v: v.reshape(1, -1).astype(F32)

    w_main = w_in[:, :5 * width].astype(BF16)
    w_fl = jnp.pad(w_in[:, 5 * width:], ((0, 0), (0, LANES - FOX_HEADS))).astype(BF16)
    b_fl = jnp.pad(fox_bf.astype(F32), (0, LANES - FOX_HEADS)).reshape(1, LANES)
    t_blk = FOX_BLOCK
    xr, gr, qt, k, vt, f_t = _inproj(x, row(norm1_g), w_main, w_fl, b_fl, tm=t_blk)

    w_gate = jnp.concatenate([_block_diag(lru_wa), _block_diag(lru_wx)], axis=1).astype(BF16)
    b_gate = jnp.concatenate([lru_ba, lru_bx]).reshape(1, -1).astype(F32)
    y_lru = _rglru(xr, gr, conv_w.astype(F32), row(conv_b), w_gate, b_gate,
                   row(lru_lambda), row(gn_lru_g), ts=256)

    pairs = FOX_HEADS // 2
    f_pair = f_t.reshape(bsz, pairs, 2, seq)
    f_col = f_pair.transpose(0, 1, 3, 2)
    f_row = f_pair.reshape(bsz, pairs, 2, seq // t_blk, t_blk).transpose(0, 1, 3, 2, 4)
    y_fox_t = _fox(qt, k, vt, f_row, f_col)

    t = bsz * seq
    n_hp = 2 * PEER_HEADS
    subkeys = peer_subkeys.reshape(n_hp, PEER_NKEYS, -1).astype(BF16)
    x1, h2t, s1, s2 = _mid(x.reshape(t, d), y_lru.reshape(t, width), y_fox_t,
                           row(gn_fox_g), w_out.astype(BF16), row(norm2_g),
                           peer_wq.T.astype(BF16), subkeys, tm=256)

    n_sel, p1, rank2, p2 = _gates(s1, s2, tm=256)
    out = _peer(h2t, peer_u.astype(BF16), peer_v.T.astype(BF16), rank2, p2, n_sel, p1, x1,
                row(out_g), tm=512, te=2048)
    return out.reshape(bsz, seq, d)


def kernel(x, norm1_g, w_in, conv_w, conv_b, lru_wa, lru_ba, lru_wx, lru_bx, lru_lambda,
           fox_bf, gn_lru_g, gn_fox_g, w_out, norm2_g, peer_wq, peer_subkeys, peer_u,
           peer_v, final_g):
    depth = w_in.shape[0]
    assert depth == 1, "the fused final RMSNorm assumes a single trunk layer"
    return _layer(x, norm1_g[0], w_in[0], conv_w[0], conv_b[0], lru_wa[0], lru_ba[0],
                  lru_wx[0], lru_bx[0], lru_lambda[0], fox_bf[0], gn_lru_g[0], gn_fox_g[0],
                  w_out[0], norm2_g[0], peer_wq[0], peer_subkeys[0], peer_u[0], peer_v[0],
                  final_g)
```

```python
import functools

import jax
import jax.numpy as jnp
import numpy as np
from jax import lax
from jax.experimental import pallas as pl
from jax.experimental.pallas import tpu as pltpu

F32 = jnp.float32
BF16 = jnp.bfloat16

RMS_EPS = 1e-6
LRU_C = 8.0
CONV_WIDTH = 4
LRU_HEADS = 8
FOX_HEADS = 8
PEER_HEADS = 8
PEER_NKEYS = 128
PEER_TOPK = 16
FOX_BLOCK = 512

LANES = 128
SUBLANES = 8
VMEM_LIMIT = 56 * 1024 * 1024

NEG_INF = float("-inf")
LOG2E = 1.4426950408889634
_GELU_K = 0.7978845608028654
BF16_ROWS = 16


def _rms(x, g):
    return x * lax.rsqrt(jnp.mean(x * x, axis=-1, keepdims=True) + RMS_EPS) * g


def _gelu_tanh_twice(x):
    inner = x * (_GELU_K + (_GELU_K * 0.044715) * (x * x))
    return x + x * jnp.tanh(inner)


def _inproj_body(x_ref, g_ref, w_ref, wfl_ref, bf_ref,
                 xr_ref, gr_ref, qt_ref, k_ref, vt_ref, ft_ref, carry_ref,
                 *, tm, width, q_scale):
    s = pl.program_id(1)

    @pl.when(s == 0)
    def _():
        carry_ref[...] = jnp.zeros_like(carry_ref)

    hb = _rms(x_ref[0], g_ref[...]).astype(BF16)

    def piece(i):
        return jnp.dot(hb, w_ref[:, i * width:(i + 1) * width],
                       preferred_element_type=F32)

    xr_ref[0] = piece(0)
    gr_ref[0] = piece(1)
    qt_ref[0, 0] = (piece(2) * (q_scale * LOG2E)).T.astype(BF16)
    k_ref[0] = piece(3).astype(BF16)
    vt_ref[0, 0] = piece(4).T.astype(BF16)

    fl = jnp.dot(hb, wfl_ref[...], preferred_element_type=F32)
    c = jax.nn.log_sigmoid(fl + bf_ref[...]) * LOG2E
    row = lax.broadcasted_iota(jnp.int32, c.shape, 0)
    sh = 1
    while sh < tm:
        c = c + jnp.where(row >= sh, pltpu.roll(c, sh, axis=0), 0.0)
        sh *= 2
    f = c + carry_ref[...]
    carry_ref[...] = f[tm - 1:tm, :]
    ft_ref[0] = f.T[0:FOX_HEADS, :]


def _inproj(x, g, w_main, w_fl, b_fl, *, tm):
    bsz, seq, d = x.shape
    width = w_main.shape[1] // 5
    body = functools.partial(_inproj_body, tm=tm, width=width,
                             q_scale=(width // FOX_HEADS) ** -0.5)
    tok = lambda b, s: (b, s, 0)
    const = lambda b, s: (0, 0)
    big = jax.ShapeDtypeStruct((bsz, seq, width), F32)
    bigb = jax.ShapeDtypeStruct((bsz, seq, width), BF16)
    blocked_t = jax.ShapeDtypeStruct((bsz, seq // tm, width, tm), BF16)
    return pl.pallas_call(
        body,
        grid=(bsz, seq // tm),
        in_specs=[
            pl.BlockSpec((1, tm, d), tok),
            pl.BlockSpec((1, d), const),
            pl.BlockSpec(w_main.shape, const),
            pl.BlockSpec(w_fl.shape, const),
            pl.BlockSpec((1, LANES), const),
        ],
        out_specs=[pl.BlockSpec((1, tm, width), tok),
                   pl.BlockSpec((1, tm, width), tok),
                   pl.BlockSpec((1, 1, width, tm), lambda b, s: (b, s, 0, 0)),
                   pl.BlockSpec((1, tm, width), tok),
                   pl.BlockSpec((1, 1, width, tm), lambda b, s: (b, s, 0, 0)),
                   pl.BlockSpec((1, FOX_HEADS, tm), lambda b, s: (b, 0, s))],
        out_shape=[big, big, blocked_t, bigb, blocked_t,
                   jax.ShapeDtypeStruct((bsz, FOX_HEADS, seq), F32)],
        scratch_shapes=[pltpu.VMEM((1, LANES), F32)],
        compiler_params=pltpu.CompilerParams(
            dimension_semantics=("parallel", "arbitrary"),
            vmem_limit_bytes=VMEM_LIMIT),
        name="inproj",
    )(x, g, w_main, w_fl, b_fl)


def _rglru_body(xr_ref, gr_ref, cw_ref, cb_ref, wg_ref, bg_ref, lam_ref, gn_ref,
                y_ref, xbuf_ref, hc_ref, *, ts, width):
    s = pl.program_id(1)

    @pl.when(s == 0)
    def _():
        xbuf_ref[0:SUBLANES, :] = jnp.zeros((SUBLANES, width), F32)
        hc_ref[...] = jnp.zeros_like(hc_ref)

    x = xr_ref[0]
    xbuf_ref[SUBLANES:SUBLANES + ts, :] = x
    cw = cw_ref[...]
    conv = cb_ref[...] + cw[CONV_WIDTH - 1:CONV_WIDTH, :] * x
    for back in range(1, CONV_WIDTH):
        k = CONV_WIDTH - 1 - back
        conv = conv + cw[k:k + 1, :] * xbuf_ref[SUBLANES - back:SUBLANES - back + ts, :]
    xbuf_ref[0:SUBLANES, :] = x[ts - SUBLANES:ts, :]

    gates = jnp.dot(conv.astype(BF16), wg_ref[...],
                    preferred_element_type=F32) + bg_ref[...]
    r = jax.nn.sigmoid(gates[:, :width])
    i = jax.nn.sigmoid(gates[:, width:])
    log_a = (-LRU_C) * r * jax.nn.softplus(-lam_ref[...])
    a = jnp.exp(log_a)
    u = jnp.sqrt(1.0 - a * a) * (i * conv)

    row = lax.broadcasted_iota(jnp.int32, a.shape, 0)
    sh = 1
    while sh < ts:
        keep = row >= sh
        a_s = jnp.where(keep, pltpu.roll(a, sh, axis=0), 1.0)
        u_s = jnp.where(keep, pltpu.roll(u, sh, axis=0), 0.0)
        u = a * u_s + u
        a = a * a_s
        sh *= 2
    h = u + a * hc_ref[...]
    hc_ref[...] = h[ts - 1:ts, :]

    y = h * jax.nn.gelu(gr_ref[0])
    y_ref[0] = _rms(y, gn_ref[...]).astype(BF16)


def _rglru(xr, gr, conv_w, conv_b, w_gate, b_gate, lam, gn, *, ts):
    bsz, seq, width = xr.shape
    body = functools.partial(_rglru_body, ts=ts, width=width)
    tok = lambda b, s: (b, s, 0)
    const = lambda b, s: (0, 0)
    return pl.pallas_call(
        body,
        grid=(bsz, seq // ts),
        in_specs=[
            pl.BlockSpec((1, ts, width), tok),
            pl.BlockSpec((1, ts, width), tok),
            pl.BlockSpec(conv_w.shape, const),
            pl.BlockSpec((1, width), const),
            pl.BlockSpec(w_gate.shape, const),
            pl.BlockSpec((1, 2 * width), const),
            pl.BlockSpec((1, width), const),
            pl.BlockSpec((1, width), const),
        ],
        out_specs=pl.BlockSpec((1, ts, width), tok),
        out_shape=jax.ShapeDtypeStruct((bsz, seq, width), BF16),
        scratch_shapes=[pltpu.VMEM((ts + SUBLANES, width), F32),
                        pltpu.VMEM((1, width), F32)],
        compiler_params=pltpu.CompilerParams(
            dimension_semantics=("parallel", "arbitrary"),
            vmem_limit_bytes=VMEM_LIMIT),
        name="rglru",
    )(xr, gr, conv_w, conv_b, w_gate, b_gate, lam, gn)


_ST_M, _ST_L = 0, 2
_PB_ALPHA, _PB_LFIN = 0, 4


def _fox_body(qi_off_ref, kj_off_ref, qi_diag_ref, qt_ref, k_ref, vt_ref, fq_ref, fk_ref,
              o_ref, s0_ref, s1_ref, p0_ref, p1_ref, p2_ref, p3_ref, st_ref, pbo_ref, pbd_ref,
              acc_ref, mb_ref,
              *, n_off, n_diag, tq, tk, dh):
    row_t = lax.broadcasted_iota(jnp.int32, (tk, tq), 0)
    col_t = lax.broadcasted_iota(jnp.int32, (tk, tq), 1)
    mb_ref[...] = jnp.where(row_t <= col_t, 0.0, NEG_INF)
    st_ref[:, _ST_M:_ST_M + 2, :] = jnp.full((st_ref.shape[0], 2, tq), NEG_INF, F32)
    st_ref[:, _ST_L:_ST_L + 2, :] = jnp.zeros((st_ref.shape[0], 2, tq), F32)
    acc_ref[...] = jnp.zeros_like(acc_ref)
    head_row = lax.broadcasted_iota(jnp.int32, (2 * dh, 1), 0)

    def stage_a(qi, kj, s_ref):
        qt = qt_ref[0, qi]
        kb = k_ref[0, pl.ds(pl.multiple_of(kj * tk, tk), tk), :]
        for h in range(2):
            in_head = (head_row >= h * dh) & (head_row < (h + 1) * dh)
            qm = jnp.where(in_head, qt, jnp.zeros_like(qt))
            s_ref[h] = jnp.dot(kb, qm, preferred_element_type=F32)

    def stage_b(qi, kj, s_ref, p_ref, pb_ref, buf, diagonal):
        fq2 = fq_ref[0, 0, qi]
        for h in range(2):
            fk = fk_ref[0, 0, pl.ds(pl.multiple_of(kj * tk, tk), tk), h:h + 1]
            fq = fq2[h:h + 1, :]
            z = s_ref[h] - fk
            if diagonal:
                z = z + mb_ref[...]
            mz = jnp.max(z, axis=0, keepdims=True) + fq
            m_prev = st_ref[qi, _ST_M + h:_ST_M + h + 1, :]
            m_new = jnp.maximum(m_prev, mz)
            alpha = jnp.exp2(m_prev - m_new)
            p = jnp.exp2(z - (m_new - fq))
            l_new = (alpha * st_ref[qi, _ST_L + h:_ST_L + h + 1, :]
                     + jnp.sum(p, axis=0, keepdims=True))
            p_ref[h] = p.astype(BF16)
            st_ref[qi, _ST_M + h:_ST_M + h + 1, :] = m_new
            st_ref[qi, _ST_L + h:_ST_L + h + 1, :] = l_new
            ra = _PB_ALPHA + 2 * buf + h
            pb_ref[ra:ra + 1, :] = alpha
            rl = _PB_LFIN + 2 * buf + h
            pb_ref[rl:rl + 1, :] = l_new

    def stage_c(qi, kj, p_ref, pb_ref, buf, diagonal):
        vt = vt_ref[0, kj]
        for h in range(2):
            rows = slice(h * dh, (h + 1) * dh)
            pv = jnp.dot(vt[rows, :], p_ref[h], preferred_element_type=F32)
            ra = _PB_ALPHA + 2 * buf + h
            acc = pb_ref[ra:ra + 1, :] * acc_ref[qi, rows, :] + pv
            acc_ref[qi, rows, :] = acc
            if diagonal:
                rl = _PB_LFIN + 2 * buf + h
                o_ref[0, qi, rows, :] = acc / pb_ref[rl:rl + 1, :]

    def run(n_items, item, diagonal, pa_ref, pz_ref, pb_ref):
        stage_a(*item(0), s0_ref)

        def pair(jj, carry):
            j0 = 2 * jj
            stage_a(*item(j0 + 1), s1_ref)
            stage_b(*item(j0), s0_ref, pa_ref, pb_ref, 0, diagonal)
            stage_c(*item(jnp.maximum(j0 - 1, 0)), pz_ref, pb_ref, 1, diagonal)
            stage_a(*item(jnp.minimum(j0 + 2, n_items - 1)), s0_ref)
            stage_b(*item(j0 + 1), s1_ref, pz_ref, pb_ref, 1, diagonal)
            stage_c(*item(j0), pa_ref, pb_ref, 0, diagonal)
            return carry

        lax.fori_loop(0, n_items // 2, pair, 0)
        stage_c(*item(n_items - 1), pz_ref, pb_ref, 1, diagonal)

    p1_ref[...] = jnp.zeros_like(p1_ref)
    p3_ref[...] = jnp.zeros_like(p3_ref)
    pbo_ref[...] = jnp.ones_like(pbo_ref)
    pbd_ref[...] = jnp.ones_like(pbd_ref)
    run(n_off, lambda j: (qi_off_ref[j], kj_off_ref[j]), False, p0_ref, p1_ref, pbo_ref)
    run(n_diag, lambda j: (qi_diag_ref[j], qi_diag_ref[j]), True, p2_ref, p3_ref, pbd_ref)


def _fox(qt, k, vt, f_row, f_col):
    bsz, nq, width, tq = qt.shape
    seq = k.shape[1]
    dh = width // FOX_HEADS
    pairs = FOX_HEADS // 2
    qi_off = np.concatenate([np.full(i, i, np.int32) for i in range(1, nq)])
    kj_off = np.concatenate([np.arange(i, dtype=np.int32) for i in range(1, nq)])
    qi_diag = np.arange(nq, dtype=np.int32)
    assert len(qi_off) % 2 == 0 and nq % 2 == 0, "the pipelined loops take two items per trip"
    body = functools.partial(_fox_body, n_off=len(qi_off), n_diag=nq, tq=tq, tk=tq, dh=dh)
    blk_t = pl.BlockSpec((1, nq, 2 * dh, tq), lambda b, p, *_: (b, 0, p, 0))
    grid_spec = pltpu.PrefetchScalarGridSpec(
        num_scalar_prefetch=3,
        grid=(bsz, pairs),
        in_specs=[
            blk_t,
            pl.BlockSpec((1, seq, 2 * dh), lambda b, p, *_: (b, 0, p)),
            blk_t,
            pl.BlockSpec((1, 1, nq, 2, tq), lambda b, p, *_: (b, p, 0, 0, 0)),
            pl.BlockSpec((1, 1, seq, 2), lambda b, p, *_: (b, p, 0, 0)),
        ],
        out_specs=blk_t,
        scratch_shapes=[pltpu.VMEM((2, tq, tq), F32),
                        pltpu.VMEM((2, tq, tq), F32),
                        pltpu.VMEM((2, tq, tq), BF16),
                        pltpu.VMEM((2, tq, tq), BF16),
                        pltpu.VMEM((2, tq, tq), BF16),
                        pltpu.VMEM((2, tq, tq), BF16),
                        pltpu.VMEM((nq, SUBLANES, tq), F32),
                        pltpu.VMEM((SUBLANES, tq), F32),
                        pltpu.VMEM((SUBLANES, tq), F32),
                        pltpu.VMEM((nq, 2 * dh, tq), F32),
                        pltpu.VMEM((tq, tq), F32)],
    )
    return pl.pallas_call(
        body,
        grid_spec=grid_spec,
        out_shape=jax.ShapeDtypeStruct((bsz, nq, width, tq), F32),
        compiler_params=pltpu.CompilerParams(
            dimension_semantics=("parallel", "parallel"),
            vmem_limit_bytes=VMEM_LIMIT),
        name="fox",
    )(jnp.asarray(qi_off), jnp.asarray(kj_off), jnp.asarray(qi_diag), qt, k, vt, f_row, f_col)


def _mid_body(x_ref, yl_ref, yf_ref, gnf_ref, wo_ref, g2_ref, wqt_ref, sk_ref,
              x1_ref, h2t_ref, s1_ref, s2_ref, *, half):
    yfn = _rms(yf_ref[0, 0].T, gnf_ref[...]).astype(BF16)
    mix = jnp.dot(yl_ref[...], wo_ref[0:half, :], preferred_element_type=F32)
    mix = mix + jnp.dot(yfn, wo_ref[half:2 * half, :], preferred_element_type=F32)
    x1 = x_ref[...] + mix
    x1_ref[...] = x1
    h2t = _rms(x1, g2_ref[...]).T.astype(BF16)
    h2t_ref[...] = h2t
    qpt = jnp.dot(wqt_ref[...], h2t, preferred_element_type=F32)
    for hp in range(2 * PEER_HEADS):
        qs = qpt[hp * LANES:(hp + 1) * LANES, :].astype(BF16)
        sc = jnp.dot(sk_ref[hp], qs, preferred_element_type=F32)
        if hp % 2 == 0:
            s1_ref[hp // 2] = sc
        else:
            s2_ref[hp // 2] = sc


def _mid(x2d, y_lru, y_fox_t, gn_fox, w_out, g2, wq_t, subkeys, *, tm):
    t, d = x2d.shape
    half = y_lru.shape[1]
    _, n_blk, _, t_blk = y_fox_t.shape
    per_blk = t_blk // tm
    body = functools.partial(_mid_body, half=half)
    tok = lambda i: (i, 0)
    const2 = lambda i: (0, 0)
    sc_shape = jax.ShapeDtypeStruct((PEER_HEADS, PEER_NKEYS, t), F32)
    sc_spec = pl.BlockSpec((PEER_HEADS, PEER_NKEYS, tm), lambda i: (0, 0, i))
    return pl.pallas_call(
        body,
        grid=(t // tm,),
        in_specs=[
            pl.BlockSpec((tm, d), tok),
            pl.BlockSpec((tm, half), tok),
            pl.BlockSpec((1, 1, half, tm),
                         lambda i: (i // (per_blk * n_blk), (i // per_blk) % n_blk, 0,
                                    i % per_blk)),
            pl.BlockSpec((1, half), const2),
            pl.BlockSpec(w_out.shape, const2),
            pl.BlockSpec((1, d), const2),
            pl.BlockSpec(wq_t.shape, const2),
            pl.BlockSpec(subkeys.shape, lambda i: (0, 0, 0)),
        ],
        out_specs=[pl.BlockSpec((tm, d), tok),
                   pl.BlockSpec((d, tm), lambda i: (0, i)),
                   sc_spec, sc_spec],
        out_shape=[jax.ShapeDtypeStruct((t, d), F32),
                   jax.ShapeDtypeStruct((d, t), BF16),
                   sc_shape, sc_shape],
        compiler_params=pltpu.CompilerParams(
            dimension_semantics=("parallel",),
            vmem_limit_bytes=VMEM_LIMIT),
        name="mid",
    )(x2d, y_lru, y_fox_t, gn_fox, w_out, g2, wq_t, subkeys)


_CAND = [(a, b) for a in range(PEER_TOPK) for b in range(PEER_TOPK)
         if (a + 1) * (b + 1) <= PEER_TOPK]
_CAND_ROWS = -(-len(_CAND) // SUBLANES) * SUBLANES


def _bf16_pair_word(x):
    bits = pltpu.bitcast(x.astype(BF16).astype(F32), jnp.uint32)
    return bits | (bits >> 16)


def _top_values(x, n, want_rank=False):
    vals = []
    rank = jnp.full(x.shape, float(n), F32) if want_rank else None
    for r in range(n):
        mx = jnp.max(x, axis=0, keepdims=True)
        vals.append(mx)
        hit = x == mx
        x = jnp.where(hit, NEG_INF, x)
        if want_rank:
            rank = jnp.where(hit, float(r), rank)
    return vals, rank


def _gates_body(s1_ref, s2_ref, n_ref, p1_ref, rank_ref, p2_ref, cand_ref):
    s1 = s1_ref[0]
    s2 = s2_ref[0]
    a, _ = _top_values(s1, PEER_TOPK)
    b, rank = _top_values(s2, PEER_TOPK, want_rank=True)
    cand_ref[...] = jnp.full_like(cand_ref, NEG_INF)
    sums = [a[ia] + b[ib] for ia, ib in _CAND]
    for r, v in enumerate(sums):
        cand_ref[r:r + 1, :] = v
    c, _ = _top_values(cand_ref[...], PEER_TOPK)
    z = jnp.zeros_like(c[0])
    for r in range(PEER_TOPK):
        z = z + jnp.exp(c[r] - c[0])
    tau = c[PEER_TOPK - 1]
    picked = [jnp.zeros_like(tau) for _ in range(PEER_TOPK)]
    for (ia, _), v in zip(_CAND, sums):
        picked[ia] = picked[ia] + jnp.where(v >= tau, 1.0, 0.0)
    n = jnp.zeros_like(s1)
    for ia in range(PEER_TOPK):
        n = jnp.where(s1 == a[ia], picked[ia], n)
    n_ref[0] = _bf16_pair_word(n)
    p1_ref[0] = _bf16_pair_word(0.5 * jnp.exp(s1 - a[0]) / z)
    rank_ref[0] = rank.astype(BF16)
    p2_ref[0] = jnp.exp(s2 - b[0]).astype(BF16)


def _gates(s1, s2, *, tm):
    heads, nk, t = s1.shape
    spec = pl.BlockSpec((1, nk, tm), lambda i, h: (h, 0, i))
    shape = jax.ShapeDtypeStruct(s1.shape, jnp.uint32)
    shape_b = jax.ShapeDtypeStruct(s1.shape, BF16)
    return pl.pallas_call(
        _gates_body,
        grid=(t // tm, heads),
        in_specs=[spec, spec],
        out_specs=[spec, spec, spec, spec],
        out_shape=[shape, shape, shape_b, shape_b],
        scratch_shapes=[pltpu.VMEM((_CAND_ROWS, tm), F32)],
        compiler_params=pltpu.CompilerParams(
            dimension_semantics=("parallel", "parallel"),
            vmem_limit_bytes=VMEM_LIMIT),
        name="gates",
    )(s1, s2)


def _peer_body(h2_ref, u_ref, vt_ref, rank_ref, p2_ref, n_ref, p1_ref, x1_ref, gf_ref,
               o_ref, pre_ref, acc_ref, *, per_tile, rows_per_item):
    s = pl.program_id(0)
    n = s - 1

    @pl.when(s == 0)
    def _():
        pre_ref[...] = jnp.zeros_like(pre_ref)
        acc_ref[...] = jnp.zeros_like(acc_ref)

    pre_next = jnp.dot(u_ref[...], h2_ref[...], preferred_element_type=F32)

    tm = pre_ref.shape[1]

    def packed_row(ref, h, r):
        words = jnp.broadcast_to(ref[h, r:r + 1, :], (SUBLANES, tm))
        return pltpu.bitcast(words, BF16)[None]

    chunks = []
    groups = PEER_NKEYS // BF16_ROWS
    for r in range(rows_per_item):
        w = jnp.zeros((groups, BF16_ROWS, tm), BF16)
        for h in range(PEER_HEADS):
            rank = rank_ref[h].reshape(groups, BF16_ROWS, tm)
            p2 = p2_ref[h].reshape(groups, BF16_ROWS, tm)
            sel = jnp.where(rank < packed_row(n_ref, h, r), p2, jnp.zeros_like(p2))
            w = w + sel * packed_row(p1_ref, h, r)
        rows = slice(r * PEER_NKEYS, (r + 1) * PEER_NKEYS)
        act = _gelu_tanh_twice(pre_ref[rows, :]) * w.reshape(PEER_NKEYS, tm).astype(F32)
        chunks.append(act.astype(BF16))
    wact = jnp.concatenate(chunks, axis=0)

    pv = jnp.dot(vt_ref[...], wact, preferred_element_type=F32)
    first = (n % per_tile) == 0
    acc_ref[...] = jnp.where(first, pv, acc_ref[...] + pv)
    pre_ref[...] = pre_next

    @pl.when((n >= 0) & (n % per_tile == per_tile - 1))
    def _():
        xo = x1_ref[...] + acc_ref[...].T
        o_ref[...] = _rms(xo, gf_ref[...])


def _peer(h2t, u, vt, rank, p2, n_word, p1_word, x1, gf, *, tm, te):
    d, t = h2t.shape
    n_exp = u.shape[0]
    per_tile = n_exp // te
    rows_per_item = te // PEER_NKEYS
    n_items = (t // tm) * per_tile
    body = functools.partial(_peer_body, per_tile=per_tile, rows_per_item=rows_per_item)

    def item(s, off):
        return jnp.clip(s - 1 + off, 0, n_items - 1)

    tile = lambda s, off: item(s, off) // per_tile
    blk = lambda s, off: item(s, off) % per_tile
    full_spec = pl.BlockSpec((PEER_HEADS, PEER_NKEYS, tm), lambda s: (0, 0, tile(s, 0)))
    rows_spec = pl.BlockSpec((PEER_HEADS, rows_per_item, tm),
                             lambda s: (0, blk(s, 0), tile(s, 0)))
    tok_spec = pl.BlockSpec((tm, d), lambda s: (tile(s, 0), 0))
    return pl.pallas_call(
        body,
        grid=(n_items + 1,),
        in_specs=[
            pl.BlockSpec((d, tm), lambda s: (0, tile(s, 1))),
            pl.BlockSpec((te, d), lambda s: (blk(s, 1), 0)),
            pl.BlockSpec((d, te), lambda s: (0, blk(s, 0))),
            full_spec, full_spec, rows_spec, rows_spec, tok_spec,
            pl.BlockSpec((1, d), lambda s: (0, 0)),
        ],
        out_specs=tok_spec,
        out_shape=jax.ShapeDtypeStruct((t, d), F32),
        scratch_shapes=[pltpu.VMEM((te, tm), F32), pltpu.VMEM((d, tm), F32)],
        compiler_params=pltpu.CompilerParams(
            dimension_semantics=("arbitrary",),
            vmem_limit_bytes=VMEM_LIMIT),
        name="peer",
    )(h2t, u, vt, rank, p2, n_word, p1_word, x1, gf)


def _block_diag(w):
    heads, dh, _ = w.shape
    eye = jnp.eye(heads, dtype=w.dtype)
    return jnp.einsum("hij,hg->higj", w, eye).reshape(heads * dh, heads * dh)


def _layer(x, norm1_g, w_in, conv_w, conv_b, lru_wa, lru_ba, lru_wx, lru_bx, lru_lambda,
           fox_bf, gn_lru_g, gn_fox_g, w_out, norm2_g, peer_wq, peer_subkeys, peer_u,
           peer_v, out_g):
    bsz, seq, d = x.shape
    width = lru_lambda.shape[0]
    row = lambda v: v.reshape(1, -1).astype(F32)

    w_main = w_in[:, :5 * width].astype(BF16)
    w_fl = jnp.pad(w_in[:, 5 * width:], ((0, 0), (0, LANES - FOX_HEADS))).astype(BF16)
    b_fl = jnp.pad(fox_bf.astype(F32), (0, LANES - FOX_HEADS)).reshape(1, LANES)
    t_blk = FOX_BLOCK
    xr, gr, qt, k, vt, f_t = _inproj(x, row(norm1_g), w_main, w_fl, b_fl, tm=t_blk)

    w_gate = jnp.concatenate([_block_diag(lru_wa), _block_diag(lru_wx)], axis=1).astype(BF16)
    b_gate = jnp.concatenate([lru_ba, lru_bx]).reshape(1, -1).astype(F32)
    y_lru = _rglru(xr, gr, conv_w.astype(F32), row(conv_b), w_gate, b_gate,
                   row(lru_lambda), row(gn_lru_g), ts=256)

    pairs = FOX_HEADS // 2
    f_pair = f_t.reshape(bsz, pairs, 2, seq)
    f_col = f_pair.transpose(0, 1, 3, 2)
    f_row = f_pair.reshape(bsz, pairs, 2, seq // t_blk, t_blk).transpose(0, 1, 3, 2, 4)
    y_fox_t = _fox(qt, k, vt, f_row, f_col)

    t = bsz * seq
    n_hp = 2 * PEER_HEADS
    subkeys = peer_subkeys.reshape(n_hp, PEER_NKEYS, -1).astype(BF16)
    x1, h2t, s1, s2 = _mid(x.reshape(t, d), y_lru.reshape(t, width), y_fox_t,
                           row(gn_fox_g), w_out.astype(BF16), row(norm2_g),
                           peer_wq.T.astype(BF16), subkeys, tm=512)

    n_sel, p1, rank2, p2 = _gates(s1, s2, tm=256)
    out = _peer(h2t, peer_u.astype(BF16), peer_v.T.astype(BF16), rank2, p2, n_sel, p1, x1,
                row(out_g), tm=512, te=2048)
    return out.reshape(bsz, seq, d)


def kernel(x, norm1_g, w_in, conv_w, conv_b, lru_wa, lru_ba, lru_wx, lru_bx, lru_lambda,
           fox_bf, gn_lru_g, gn_fox_g, w_out, norm2_g, peer_wq, peer_subkeys, peer_u,
           peer_v, final_g):
    depth = w_in.shape[0]
    assert depth == 1, "the fused final RMSNorm assumes a single trunk layer"
    return _layer(x, norm1_g[0], w_in[0], conv_w[0], conv_b[0], lru_wa[0], lru_ba[0],
                  lru_wx[0], lru_bx[0], lru_lambda[0], fox_bf[0], gn_lru_g[0], gn_fox_g[0],
                  w_out[0], norm2_g[0], peer_wq[0], peer_subkeys[0], peer_u[0], peer_v[0],
                  final_g)
```

```python
import functools

import jax
import jax.numpy as jnp
import numpy as np
from jax import lax
from jax.experimental import pallas as pl
from jax.experimental.pallas import tpu as pltpu

F32 = jnp.float32
BF16 = jnp.bfloat16

RMS_EPS = 1e-6
LRU_C = 8.0
CONV_WIDTH = 4
LRU_HEADS = 8
FOX_HEADS = 8
PEER_HEADS = 8
PEER_NKEYS = 128
PEER_TOPK = 16
FOX_BLOCK = 512

LANES = 128
SUBLANES = 8
VMEM_LIMIT = 56 * 1024 * 1024

NEG_INF = float("-inf")
LOG2E = 1.4426950408889634
_GELU_K = 0.7978845608028654
BF16_ROWS = 16


def _rms(x, g):
    return x * lax.rsqrt(jnp.mean(x * x, axis=-1, keepdims=True) + RMS_EPS) * g


def _gelu_tanh_factor(x):
    return jnp.tanh(x * (_GELU_K + (_GELU_K * 0.044715) * (x * x)))


def _inproj_body(x_ref, g_ref, w_ref, wfl_ref, bf_ref,
                 xr_ref, gr_ref, qt_ref, k_ref, vt_ref, ft_ref, carry_ref,
                 *, tm, width, q_scale):
    s = pl.program_id(1)

    @pl.when(s == 0)
    def _():
        carry_ref[...] = jnp.zeros_like(carry_ref)

    hb = _rms(x_ref[0], g_ref[...]).astype(BF16)

    def piece(i):
        return jnp.dot(hb, w_ref[:, i * width:(i + 1) * width],
                       preferred_element_type=F32)

    xr_ref[0] = piece(0)
    gr_ref[0] = piece(1)
    qt_ref[0, 0] = (piece(2) * (q_scale * LOG2E)).T.astype(BF16)
    k_ref[0] = piece(3).astype(BF16)
    vt_ref[0, 0] = piece(4).T.astype(BF16)

    fl = jnp.dot(hb, wfl_ref[...], preferred_element_type=F32)
    c = jax.nn.log_sigmoid(fl + bf_ref[...]) * LOG2E
    row = lax.broadcasted_iota(jnp.int32, c.shape, 0)
    sh = 1
    while sh < tm:
        c = c + jnp.where(row >= sh, pltpu.roll(c, sh, axis=0), 0.0)
        sh *= 2
    f = c + carry_ref[...]
    carry_ref[...] = f[tm - 1:tm, :]
    ft_ref[0] = f.T[0:FOX_HEADS, :]


def _inproj(x, g, w_main, w_fl, b_fl, *, tm):
    bsz, seq, d = x.shape
    width = w_main.shape[1] // 5
    body = functools.partial(_inproj_body, tm=tm, width=width,
                             q_scale=(width // FOX_HEADS) ** -0.5)
    tok = lambda b, s: (b, s, 0)
    const = lambda b, s: (0, 0)
    big = jax.ShapeDtypeStruct((bsz, seq, width), F32)
    bigb = jax.ShapeDtypeStruct((bsz, seq, width), BF16)
    blocked_t = jax.ShapeDtypeStruct((bsz, seq // tm, width, tm), BF16)
    return pl.pallas_call(
        body,
        grid=(bsz, seq // tm),
        in_specs=[
            pl.BlockSpec((1, tm, d), tok),
            pl.BlockSpec((1, d), const),
            pl.BlockSpec(w_main.shape, const),
            pl.BlockSpec(w_fl.shape, const),
            pl.BlockSpec((1, LANES), const),
        ],
        out_specs=[pl.BlockSpec((1, tm, width), tok),
                   pl.BlockSpec((1, tm, width), tok),
                   pl.BlockSpec((1, 1, width, tm), lambda b, s: (b, s, 0, 0)),
                   pl.BlockSpec((1, tm, width), tok),
                   pl.BlockSpec((1, 1, width, tm), lambda b, s: (b, s, 0, 0)),
                   pl.BlockSpec((1, FOX_HEADS, tm), lambda b, s: (b, 0, s))],
        out_shape=[big, big, blocked_t, bigb, blocked_t,
                   jax.ShapeDtypeStruct((bsz, FOX_HEADS, seq), F32)],
        scratch_shapes=[pltpu.VMEM((1, LANES), F32)],
        compiler_params=pltpu.CompilerParams(
            dimension_semantics=("parallel", "arbitrary"),
            vmem_limit_bytes=VMEM_LIMIT),
        name="inproj",
    )(x, g, w_main, w_fl, b_fl)


def _rglru_body(xr_ref, gr_ref, cw_ref, cb_ref, wg_ref, bg_ref, lam_ref, gn_ref,
                y_ref, xbuf_ref, hc_ref, *, ts, width):
    s = pl.program_id(1)

    @pl.when(s == 0)
    def _():
        xbuf_ref[0:SUBLANES, :] = jnp.zeros((SUBLANES, width), F32)
        hc_ref[...] = jnp.zeros_like(hc_ref)

    x = xr_ref[0]
    xbuf_ref[SUBLANES:SUBLANES + ts, :] = x
    cw = cw_ref[...]
    conv = cb_ref[...] + cw[CONV_WIDTH - 1:CONV_WIDTH, :] * x
    for back in range(1, CONV_WIDTH):
        k = CONV_WIDTH - 1 - back
        conv = conv + cw[k:k + 1, :] * xbuf_ref[SUBLANES - back:SUBLANES - back + ts, :]
    xbuf_ref[0:SUBLANES, :] = x[ts - SUBLANES:ts, :]

    gates = jnp.dot(conv.astype(BF16), wg_ref[...],
                    preferred_element_type=F32) + bg_ref[...]
    r = jax.nn.sigmoid(gates[:, :width])
    i = jax.nn.sigmoid(gates[:, width:])
    log_a = (-LRU_C) * r * jax.nn.softplus(-lam_ref[...])
    a = jnp.exp(log_a)
    u = jnp.sqrt(1.0 - a * a) * (i * conv)

    row = lax.broadcasted_iota(jnp.int32, a.shape, 0)
    sh = 1
    while sh < ts:
        keep = row >= sh
        a_s = jnp.where(keep, pltpu.roll(a, sh, axis=0), 1.0)
        u_s = jnp.where(keep, pltpu.roll(u, sh, axis=0), 0.0)
        u = a * u_s + u
        a = a * a_s
        sh *= 2
    h = u + a * hc_ref[...]
    hc_ref[...] = h[ts - 1:ts, :]

    y = h * jax.nn.gelu(gr_ref[0])
    y_ref[0] = _rms(y, gn_ref[...]).astype(BF16)


def _rglru(xr, gr, conv_w, conv_b, w_gate, b_gate, lam, gn, *, ts):
    bsz, seq, width = xr.shape
    body = functools.partial(_rglru_body, ts=ts, width=width)
    tok = lambda b, s: (b, s, 0)
    const = lambda b, s: (0, 0)
    return pl.pallas_call(
        body,
        grid=(bsz, seq // ts),
        in_specs=[
            pl.BlockSpec((1, ts, width), tok),
            pl.BlockSpec((1, ts, width), tok),
            pl.BlockSpec(conv_w.shape, const),
            pl.BlockSpec((1, width), const),
            pl.BlockSpec(w_gate.shape, const),
            pl.BlockSpec((1, 2 * width), const),
            pl.BlockSpec((1, width), const),
            pl.BlockSpec((1, width), const),
        ],
        out_specs=pl.BlockSpec((1, ts, width), tok),
        out_shape=jax.ShapeDtypeStruct((bsz, seq, width), BF16),
        scratch_shapes=[pltpu.VMEM((ts + SUBLANES, width), F32),
                        pltpu.VMEM((1, width), F32)],
        compiler_params=pltpu.CompilerParams(
            dimension_semantics=("parallel", "arbitrary"),
            vmem_limit_bytes=VMEM_LIMIT),
        name="rglru",
    )(xr, gr, conv_w, conv_b, w_gate, b_gate, lam, gn)


_ST_M, _ST_L = 0, 2
_PB_ALPHA, _PB_LFIN = 0, 4


def _fox_body(qi_off_ref, kj_off_ref, qi_diag_ref, qt_ref, k_ref, vt_ref, fq_ref, fk_ref,
              o_ref, s0_ref, s1_ref, p0_ref, p1_ref, p2_ref, p3_ref, st_ref, pbo_ref, pbd_ref,
              acc_ref, mb_ref,
              *, n_off, n_diag, tq, tk, dh):
    row_t = lax.broadcasted_iota(jnp.int32, (tk, tq), 0)
    col_t = lax.broadcasted_iota(jnp.int32, (tk, tq), 1)
    mb_ref[...] = jnp.where(row_t <= col_t, 0.0, NEG_INF)
    st_ref[:, _ST_M:_ST_M + 2, :] = jnp.full((st_ref.shape[0], 2, tq), NEG_INF, F32)
    st_ref[:, _ST_L:_ST_L + 2, :] = jnp.zeros((st_ref.shape[0], 2, tq), F32)
    acc_ref[...] = jnp.zeros_like(acc_ref)
    head_row = lax.broadcasted_iota(jnp.int32, (2 * dh, 1), 0)

    def stage_a(qi, kj, s_ref):
        qt = qt_ref[0, qi]
        kb = k_ref[0, pl.ds(pl.multiple_of(kj * tk, tk), tk), :]
        for h in range(2):
            in_head = (head_row >= h * dh) & (head_row < (h + 1) * dh)
            qm = jnp.where(in_head, qt, jnp.zeros_like(qt))
            s_ref[h] = jnp.dot(kb, qm, preferred_element_type=F32)

    def stage_b(qi, kj, s_ref, p_ref, pb_ref, buf, diagonal):
        fq2 = fq_ref[0, 0, qi]
        for h in range(2):
            fk = fk_ref[0, 0, pl.ds(pl.multiple_of(kj * tk, tk), tk), h:h + 1]
            fq = fq2[h:h + 1, :]
            z = s_ref[h] - fk
            if diagonal:
                z = z + mb_ref[...]
            mz = jnp.max(z, axis=0, keepdims=True) + fq
            m_prev = st_ref[qi, _ST_M + h:_ST_M + h + 1, :]
            m_new = jnp.maximum(m_prev, mz)
            alpha = jnp.exp2(m_prev - m_new)
            p = jnp.exp2(z - (m_new - fq))
            l_new = (alpha * st_ref[qi, _ST_L + h:_ST_L + h + 1, :]
                     + jnp.sum(p, axis=0, keepdims=True))
            p_ref[h] = p.astype(BF16)
            st_ref[qi, _ST_M + h:_ST_M + h + 1, :] = m_new
            st_ref[qi, _ST_L + h:_ST_L + h + 1, :] = l_new
            ra = _PB_ALPHA + 2 * buf + h
            pb_ref[ra:ra + 1, :] = alpha
            rl = _PB_LFIN + 2 * buf + h
            pb_ref[rl:rl + 1, :] = l_new

    def stage_c(qi, kj, p_ref, pb_ref, buf, diagonal):
        vt = vt_ref[0, kj]
        for h in range(2):
            rows = slice(h * dh, (h + 1) * dh)
            pv = jnp.dot(vt[rows, :], p_ref[h], preferred_element_type=F32)
            ra = _PB_ALPHA + 2 * buf + h
            acc = pb_ref[ra:ra + 1, :] * acc_ref[qi, rows, :] + pv
            acc_ref[qi, rows, :] = acc
            if diagonal:
                rl = _PB_LFIN + 2 * buf + h
                o_ref[0, qi, rows, :] = acc / pb_ref[rl:rl + 1, :]

    def run(n_items, item, diagonal, pa_ref, pz_ref, pb_ref):
        stage_a(*item(0), s0_ref)

        def pair(jj, carry):
            j0 = 2 * jj
            stage_a(*item(j0 + 1), s1_ref)
            stage_b(*item(j0), s0_ref, pa_ref, pb_ref, 0, diagonal)
            stage_c(*item(jnp.maximum(j0 - 1, 0)), pz_ref, pb_ref, 1, diagonal)
            stage_a(*item(jnp.minimum(j0 + 2, n_items - 1)), s0_ref)
            stage_b(*item(j0 + 1), s1_ref, pz_ref, pb_ref, 1, diagonal)
            stage_c(*item(j0), pa_ref, pb_ref, 0, diagonal)
            return carry

        lax.fori_loop(0, n_items // 2, pair, 0)
        stage_c(*item(n_items - 1), pz_ref, pb_ref, 1, diagonal)

    p1_ref[...] = jnp.zeros_like(p1_ref)
    p3_ref[...] = jnp.zeros_like(p3_ref)
    pbo_ref[...] = jnp.ones_like(pbo_ref)
    pbd_ref[...] = jnp.ones_like(pbd_ref)
    run(n_off, lambda j: (qi_off_ref[j], kj_off_ref[j]), False, p0_ref, p1_ref, pbo_ref)
    run(n_diag, lambda j: (qi_diag_ref[j], qi_diag_ref[j]), True, p2_ref, p3_ref, pbd_ref)


def _fox(qt, k, vt, f_row, f_col):
    bsz, nq, width, tq = qt.shape
    seq = k.shape[1]
    dh = width // FOX_HEADS
    pairs = FOX_HEADS // 2
    qi_off = np.concatenate([np.full(i, i, np.int32) for i in range(1, nq)])
    kj_off = np.concatenate([np.arange(i, dtype=np.int32) for i in range(1, nq)])
    qi_diag = np.arange(nq, dtype=np.int32)
    assert len(qi_off) % 2 == 0 and nq % 2 == 0, "the pipelined loops take two items per trip"
    body = functools.partial(_fox_body, n_off=len(qi_off), n_diag=nq, tq=tq, tk=tq, dh=dh)
    blk_t = pl.BlockSpec((1, nq, 2 * dh, tq), lambda b, p, *_: (b, 0, p, 0))
    grid_spec = pltpu.PrefetchScalarGridSpec(
        num_scalar_prefetch=3,
        grid=(bsz, pairs),
        in_specs=[
            blk_t,
            pl.BlockSpec((1, seq, 2 * dh), lambda b, p, *_: (b, 0, p)),
            blk_t,
            pl.BlockSpec((1, 1, nq, 2, tq), lambda b, p, *_: (b, p, 0, 0, 0)),
            pl.BlockSpec((1, 1, seq, 2), lambda b, p, *_: (b, p, 0, 0)),
        ],
        out_specs=blk_t,
        scratch_shapes=[pltpu.VMEM((2, tq, tq), F32),
                        pltpu.VMEM((2, tq, tq), F32),
                        pltpu.VMEM((2, tq, tq), BF16),
                        pltpu.VMEM((2, tq, tq), BF16),
                        pltpu.VMEM((2, tq, tq), BF16),
                        pltpu.VMEM((2, tq, tq), BF16),
                        pltpu.VMEM((nq, SUBLANES, tq), F32),
                        pltpu.VMEM((SUBLANES, tq), F32),
                        pltpu.VMEM((SUBLANES, tq), F32),
                        pltpu.VMEM((nq, 2 * dh, tq), F32),
                        pltpu.VMEM((tq, tq), F32)],
    )
    return pl.pallas_call(
        body,
        grid_spec=grid_spec,
        out_shape=jax.ShapeDtypeStruct((bsz, nq, width, tq), F32),
        compiler_params=pltpu.CompilerParams(
            dimension_semantics=("parallel", "parallel"),
            vmem_limit_bytes=VMEM_LIMIT),
        name="fox",
    )(jnp.asarray(qi_off), jnp.asarray(kj_off), jnp.asarray(qi_diag), qt, k, vt, f_row, f_col)


def _mid_body(x_ref, yl_ref, yf_ref, gnf_ref, wo_ref, g2_ref, wqt_ref, sk_ref,
              x1_ref, h2t_ref, s1_ref, s2_ref, *, half):
    yfn = _rms(yf_ref[0, 0].T, gnf_ref[...]).astype(BF16)
    mix = jnp.dot(yl_ref[...], wo_ref[0:half, :], preferred_element_type=F32)
    mix = mix + jnp.dot(yfn, wo_ref[half:2 * half, :], preferred_element_type=F32)
    x1 = x_ref[...] + mix
    x1_ref[...] = x1
    h2t = _rms(x1, g2_ref[...]).T.astype(BF16)
    h2t_ref[...] = h2t
    qpt = jnp.dot(wqt_ref[...], h2t, preferred_element_type=F32)
    for hp in range(2 * PEER_HEADS):
        qs = qpt[hp * LANES:(hp + 1) * LANES, :].astype(BF16)
        sc = jnp.dot(sk_ref[hp], qs, preferred_element_type=F32)
        if hp % 2 == 0:
            s1_ref[hp // 2] = sc
        else:
            s2_ref[hp // 2] = sc


def _mid(x2d, y_lru, y_fox_t, gn_fox, w_out, g2, wq_t, subkeys, *, tm):
    t, d = x2d.shape
    half = y_lru.shape[1]
    _, n_blk, _, t_blk = y_fox_t.shape
    per_blk = t_blk // tm
    body = functools.partial(_mid_body, half=half)
    tok = lambda i: (i, 0)
    const2 = lambda i: (0, 0)
    sc_shape = jax.ShapeDtypeStruct((PEER_HEADS, PEER_NKEYS, t), F32)
    sc_spec = pl.BlockSpec((PEER_HEADS, PEER_NKEYS, tm), lambda i: (0, 0, i))
    return pl.pallas_call(
        body,
        grid=(t // tm,),
        in_specs=[
            pl.BlockSpec((tm, d), tok),
            pl.BlockSpec((tm, half), tok),
            pl.BlockSpec((1, 1, half, tm),
                         lambda i: (i // (per_blk * n_blk), (i // per_blk) % n_blk, 0,
                                    i % per_blk)),
            pl.BlockSpec((1, half), const2),
            pl.BlockSpec(w_out.shape, const2),
            pl.BlockSpec((1, d), const2),
            pl.BlockSpec(wq_t.shape, const2),
            pl.BlockSpec(subkeys.shape, lambda i: (0, 0, 0)),
        ],
        out_specs=[pl.BlockSpec((tm, d), tok),
                   pl.BlockSpec((d, tm), lambda i: (0, i)),
                   sc_spec, sc_spec],
        out_shape=[jax.ShapeDtypeStruct((t, d), F32),
                   jax.ShapeDtypeStruct((d, t), BF16),
                   sc_shape, sc_shape],
        compiler_params=pltpu.CompilerParams(
            dimension_semantics=("parallel",),
            vmem_limit_bytes=VMEM_LIMIT),
        name="mid",
    )(x2d, y_lru, y_fox_t, gn_fox, w_out, g2, wq_t, subkeys)


_CAND = [(a, b) for a in range(PEER_TOPK) for b in range(PEER_TOPK)
         if (a + 1) * (b + 1) <= PEER_TOPK]
_CAND_ROWS = -(-len(_CAND) // SUBLANES) * SUBLANES


def _top_values(x, n, want_rank=False):
    vals = []
    rank = jnp.full(x.shape, float(n), F32) if want_rank else None
    for r in range(n):
        mx = jnp.max(x, axis=0, keepdims=True)
        vals.append(mx)
        hit = x == mx
        x = jnp.where(hit, NEG_INF, x)
        if want_rank:
            rank = jnp.where(hit, float(r), rank)
    return vals, rank


def _gates_body(s1_ref, s2_ref, n_ref, p1_ref, rank_ref, p2_ref, cand_ref):
    s1 = s1_ref[0]
    s2 = s2_ref[0]
    a, _ = _top_values(s1, PEER_TOPK)
    b, rank = _top_values(s2, PEER_TOPK, want_rank=True)
    cand_ref[...] = jnp.full_like(cand_ref, NEG_INF)
    sums = [a[ia] + b[ib] for ia, ib in _CAND]
    for r, v in enumerate(sums):
        cand_ref[r:r + 1, :] = v
    c, _ = _top_values(cand_ref[...], PEER_TOPK)
    z = jnp.zeros_like(c[0])
    for r in range(PEER_TOPK):
        z = z + jnp.exp(c[r] - c[0])
    tau = c[PEER_TOPK - 1]
    picked = [jnp.zeros_like(tau) for _ in range(PEER_TOPK)]
    for (ia, _), v in zip(_CAND, sums):
        picked[ia] = picked[ia] + jnp.where(v >= tau, 1.0, 0.0)
    n = jnp.zeros_like(s1)
    for ia in range(PEER_TOPK):
        n = jnp.where(s1 == a[ia], picked[ia], n)
    n_ref[0] = n
    p1_ref[0] = 0.5 * jnp.exp(s1 - a[0]) / z
    rank_ref[0] = rank.astype(BF16)
    p2_ref[0] = jnp.exp(s2 - b[0]).astype(BF16)


def _gates(s1, s2, *, tm):
    heads, nk, t = s1.shape
    spec = pl.BlockSpec((1, nk, tm), lambda i, h: (h, 0, i))
    shape = jax.ShapeDtypeStruct(s1.shape, F32)
    shape_b = jax.ShapeDtypeStruct(s1.shape, BF16)
    return pl.pallas_call(
        _gates_body,
        grid=(t // tm, heads),
        in_specs=[spec, spec],
        out_specs=[spec, spec, spec, spec],
        out_shape=[shape, shape, shape_b, shape_b],
        scratch_shapes=[pltpu.VMEM((_CAND_ROWS, tm), F32)],
        compiler_params=pltpu.CompilerParams(
            dimension_semantics=("parallel", "parallel"),
            vmem_limit_bytes=VMEM_LIMIT),
        name="gates",
    )(s1, s2)


def _peer_body(h2_ref, u_ref, vt_ref, rank_ref, p2_ref, n_ref, p1_ref, x1_ref, gf_ref,
               o_ref, pre_ref, acc_ref, *, per_tile, rows_per_item):
    s = pl.program_id(0)
    n = s - 1

    @pl.when(s == 0)
    def _():
        pre_ref[...] = jnp.zeros_like(pre_ref)
        acc_ref[...] = jnp.zeros_like(acc_ref)

    pre_next = jnp.dot(u_ref[...], h2_ref[...], preferred_element_type=F32)

    tm = pre_ref.shape[1]

    def packed_row(ref, h, r):
        tile = jnp.broadcast_to(ref[h, r:r + 1, :], (BF16_ROWS, tm)).astype(BF16)
        return tile[None]

    chunks = []
    groups = PEER_NKEYS // BF16_ROWS
    for r in range(rows_per_item):
        w = jnp.zeros((groups, BF16_ROWS, tm), BF16)
        for h in range(PEER_HEADS):
            rank = rank_ref[h].reshape(groups, BF16_ROWS, tm)
            p2 = p2_ref[h].reshape(groups, BF16_ROWS, tm)
            sel = jnp.where(rank < packed_row(n_ref, h, r), p2, jnp.zeros_like(p2))
            w = w + sel * packed_row(p1_ref, h, r)
        rows = slice(r * PEER_NKEYS, (r + 1) * PEER_NKEYS)
        x = pre_ref[rows, :]
        xb = x.astype(BF16)
        tb = _gelu_tanh_factor(x).astype(BF16)
        chunks.append((xb + xb * tb) * w.reshape(PEER_NKEYS, tm))
    wact = jnp.concatenate(chunks, axis=0)

    pv = jnp.dot(vt_ref[...], wact, preferred_element_type=F32)
    first = (n % per_tile) == 0
    acc_ref[...] = jnp.where(first, pv, acc_ref[...] + pv)
    pre_ref[...] = pre_next

    @pl.when((n >= 0) & (n % per_tile == per_tile - 1))
    def _():
        xo = x1_ref[...] + acc_ref[...].T
        o_ref[...] = _rms(xo, gf_ref[...])


def _peer(h2t, u, vt, rank, p2, n_word, p1_word, x1, gf, *, tm, te):
    d, t = h2t.shape
    n_exp = u.shape[0]
    per_tile = n_exp // te
    rows_per_item = te // PEER_NKEYS
    n_items = (t // tm) * per_tile
    body = functools.partial(_peer_body, per_tile=per_tile, rows_per_item=rows_per_item)

    def item(s, off):
        return jnp.clip(s - 1 + off, 0, n_items - 1)

    tile = lambda s, off: item(s, off) // per_tile
    blk = lambda s, off: item(s, off) % per_tile
    full_spec = pl.BlockSpec((PEER_HEADS, PEER_NKEYS, tm), lambda s: (0, 0, tile(s, 0)))
    rows_spec = pl.BlockSpec((PEER_HEADS, rows_per_item, tm),
                             lambda s: (0, blk(s, 0), tile(s, 0)))
    tok_spec = pl.BlockSpec((tm, d), lambda s: (tile(s, 0), 0))
    return pl.pallas_call(
        body,
        grid=(n_items + 1,),
        in_specs=[
            pl.BlockSpec((d, tm), lambda s: (0, tile(s, 1))),
            pl.BlockSpec((te, d), lambda s: (blk(s, 1), 0)),
            pl.BlockSpec((d, te), lambda s: (0, blk(s, 0))),
            full_spec, full_spec, rows_spec, rows_spec, tok_spec,
            pl.BlockSpec((1, d), lambda s: (0, 0)),
        ],
        out_specs=tok_spec,
        out_shape=jax.ShapeDtypeStruct((t, d), F32),
        scratch_shapes=[pltpu.VMEM((te, tm), F32), pltpu.VMEM((d, tm), F32)],
        compiler_params=pltpu.CompilerParams(
            dimension_semantics=("arbitrary",),
            vmem_limit_bytes=VMEM_LIMIT),
        name="peer",
    )(h2t, u, vt, rank, p2, n_word, p1_word, x1, gf)


def _block_diag(w):
    heads, dh, _ = w.shape
    eye = jnp.eye(heads, dtype=w.dtype)
    return jnp.einsum("hij,hg->higj", w, eye).reshape(heads * dh, heads * dh)


def _layer(x, norm1_g, w_in, conv_w, conv_b, lru_wa, lru_ba, lru_wx, lru_bx, lru_lambda,
           fox_bf, gn_lru_g, gn_fox_g, w_out, norm2_g, peer_wq, peer_subkeys, peer_u,
           peer_v, out_g):
    bsz, seq, d = x.shape
    width = lru_lambda.shape[0]
    row = lambda v: v.reshape(1, -1).astype(F32)

    w_main = w_in[:, :5 * width].astype(BF16)
    w_fl = jnp.pad(w_in[:, 5 * width:], ((0, 0), (0, LANES - FOX_HEADS))).astype(BF16)
    b_fl = jnp.pad(fox_bf.astype(F32), (0, LANES - FOX_HEADS)).reshape(1, LANES)
    t_blk = FOX_BLOCK
    xr, gr, qt, k, vt, f_t = _inproj(x, row(norm1_g), w_main, w_fl, b_fl, tm=t_blk)

    w_gate = jnp.concatenate([_block_diag(lru_wa), _block_diag(lru_wx)], axis=1).astype(BF16)
    b_gate = jnp.concatenate([lru_ba, lru_bx]).reshape(1, -1).astype(F32)
    y_lru = _rglru(xr, gr, conv_w.astype(F32), row(conv_b), w_gate, b_gate,
                   row(lru_lambda), row(gn_lru_g), ts=256)

    pairs = FOX_HEADS // 2
    f_pair = f_t.reshape(bsz, pairs, 2, seq)
    f_col = f_pair.transpose(0, 1, 3, 2)
    f_row = f_pair.reshape(bsz, pairs, 2, seq // t_blk, t_blk).transpose(0, 1, 3, 2, 4)
    y_fox_t = _fox(qt, k, vt, f_row, f_col)

    t = bsz * seq
    n_hp = 2 * PEER_HEADS
    subkeys = peer_subkeys.reshape(n_hp, PEER_NKEYS, -1).astype(BF16)
    x1, h2t, s1, s2 = _mid(x.reshape(t, d), y_lru.reshape(t, width), y_fox_t,
                           row(gn_fox_g), w_out.astype(BF16), row(norm2_g),
                           peer_wq.T.astype(BF16), subkeys, tm=512)

    n_sel, p1, rank2, p2 = _gates(s1, s2, tm=256)
    out = _peer(h2t, peer_u.astype(BF16), peer_v.T.astype(BF16), rank2, p2, n_sel, p1, x1,
                row(out_g), tm=512, te=2048)
    return out.reshape(bsz, seq, d)


def kernel(x, norm1_g, w_in, conv_w, conv_b, lru_wa, lru_ba, lru_wx, lru_bx, lru_lambda,
           fox_bf, gn_lru_g, gn_fox_g, w_out, norm2_g, peer_wq, peer_subkeys, peer_u,
           peer_v, final_g):
    depth = w_in.shape[0]
    assert depth == 1, "the fused final RMSNorm assumes a single trunk layer"
    return _layer(x, norm1_g[0], w_in[0], conv_w[0], conv_b[0], lru_wa[0], lru_ba[0],
                  lru_wx[0], lru_bx[0], lru_lambda[0], fox_bf[0], gn_lru_g[0], gn_fox_g[0],
                  w_out[0], norm2_g[0], peer_wq[0], peer_subkeys[0], peer_u[0], peer_v[0],
                  final_g)
```

```python
import functools

import jax
import jax.numpy as jnp
import numpy as np
from jax import lax
from jax.experimental import pallas as pl
from jax.experimental.pallas import tpu as pltpu

F32 = jnp.float32
BF16 = jnp.bfloat16

RMS_EPS = 1e-6
LRU_C = 8.0
CONV_WIDTH = 4
LRU_HEADS = 8
FOX_HEADS = 8
PEER_HEADS = 8
PEER_NKEYS = 128
PEER_TOPK = 16
FOX_BLOCK = 512

LANES = 128
SUBLANES = 8
VMEM_LIMIT = 56 * 1024 * 1024

NEG_INF = float("-inf")
LOG2E = 1.4426950408889634
_GELU_K = 0.7978845608028654
BF16_ROWS = 16


def _rms(x, g):
    return x * lax.rsqrt(jnp.mean(x * x, axis=-1, keepdims=True) + RMS_EPS) * g


def _gelu_tanh_factor(x):
    return jnp.tanh(x * (_GELU_K + (_GELU_K * 0.044715) * (x * x)))


def _inproj_body(x_ref, g_ref, w_ref, wfl_ref, bf_ref,
                 xr_ref, gr_ref, qt_ref, k_ref, vt_ref, ft_ref, carry_ref,
                 *, tm, width, q_scale):
    s = pl.program_id(1)

    @pl.when(s == 0)
    def _():
        carry_ref[...] = jnp.zeros_like(carry_ref)

    hb = _rms(x_ref[0], g_ref[...]).astype(BF16)

    def piece(i):
        return jnp.dot(hb, w_ref[:, i * width:(i + 1) * width],
                       preferred_element_type=F32)

    xr_ref[0] = piece(0)
    gr_ref[0] = piece(1)
    qt_ref[0, 0] = (piece(2) * (q_scale * LOG2E)).T.astype(BF16)
    k_ref[0] = piece(3).astype(BF16)
    vt_ref[0, 0] = piece(4).T.astype(BF16)

    fl = jnp.dot(hb, wfl_ref[...], preferred_element_type=F32)
    c = jax.nn.log_sigmoid(fl + bf_ref[...]) * LOG2E
    row = lax.broadcasted_iota(jnp.int32, c.shape, 0)
    sh = 1
    while sh < tm:
        c = c + jnp.where(row >= sh, pltpu.roll(c, sh, axis=0), 0.0)
        sh *= 2
    f = c + carry_ref[...]
    carry_ref[...] = f[tm - 1:tm, :]
    ft_ref[0] = f.T[0:FOX_HEADS, :]


def _inproj(x, g, w_main, w_fl, b_fl, *, tm):
    bsz, seq, d = x.shape
    width = w_main.shape[1] // 5
    body = functools.partial(_inproj_body, tm=tm, width=width,
                             q_scale=(width // FOX_HEADS) ** -0.5)
    tok = lambda b, s: (b, s, 0)
    const = lambda b, s: (0, 0)
    big = jax.ShapeDtypeStruct((bsz, seq, width), F32)
    bigb = jax.ShapeDtypeStruct((bsz, seq, width), BF16)
    blocked_t = jax.ShapeDtypeStruct((bsz, seq // tm, width, tm), BF16)
    return pl.pallas_call(
        body,
        grid=(bsz, seq // tm),
        in_specs=[
            pl.BlockSpec((1, tm, d), tok),
            pl.BlockSpec((1, d), const),
            pl.BlockSpec(w_main.shape, const),
            pl.BlockSpec(w_fl.shape, const),
            pl.BlockSpec((1, LANES), const),
        ],
        out_specs=[pl.BlockSpec((1, tm, width), tok),
                   pl.BlockSpec((1, tm, width), tok),
                   pl.BlockSpec((1, 1, width, tm), lambda b, s: (b, s, 0, 0)),
                   pl.BlockSpec((1, tm, width), tok),
                   pl.BlockSpec((1, 1, width, tm), lambda b, s: (b, s, 0, 0)),
                   pl.BlockSpec((1, FOX_HEADS, tm), lambda b, s: (b, 0, s))],
        out_shape=[big, big, blocked_t, bigb, blocked_t,
                   jax.ShapeDtypeStruct((bsz, FOX_HEADS, seq), F32)],
        scratch_shapes=[pltpu.VMEM((1, LANES), F32)],
        compiler_params=pltpu.CompilerParams(
            dimension_semantics=("parallel", "arbitrary"),
            vmem_limit_bytes=VMEM_LIMIT),
        name="inproj",
    )(x, g, w_main, w_fl, b_fl)


def _rglru_body(xr_ref, gr_ref, cw_ref, cb_ref, wg_ref, bg_ref, lam_ref, gn_ref,
                y_ref, xbuf_ref, hc_ref, *, ts, width):
    s = pl.program_id(1)

    @pl.when(s == 0)
    def _():
        xbuf_ref[0:SUBLANES, :] = jnp.zeros((SUBLANES, width), F32)
        hc_ref[...] = jnp.zeros_like(hc_ref)

    x = xr_ref[0]
    xbuf_ref[SUBLANES:SUBLANES + ts, :] = x
    cw = cw_ref[...]
    conv = cb_ref[...] + cw[CONV_WIDTH - 1:CONV_WIDTH, :] * x
    for back in range(1, CONV_WIDTH):
        k = CONV_WIDTH - 1 - back
        conv = conv + cw[k:k + 1, :] * xbuf_ref[SUBLANES - back:SUBLANES - back + ts, :]
    xbuf_ref[0:SUBLANES, :] = x[ts - SUBLANES:ts, :]

    gates = jnp.dot(conv.astype(BF16), wg_ref[...],
                    preferred_element_type=F32) + bg_ref[...]
    r = jax.nn.sigmoid(gates[:, :width])
    i = jax.nn.sigmoid(gates[:, width:])
    log_a = (-LRU_C) * r * jax.nn.softplus(-lam_ref[...])
    a = jnp.exp(log_a)
    u = jnp.sqrt(1.0 - a * a) * (i * conv)

    row = lax.broadcasted_iota(jnp.int32, a.shape, 0)
    sh = 1
    while sh < ts:
        keep = row >= sh
        a_s = jnp.where(keep, pltpu.roll(a, sh, axis=0), 1.0)
        u_s = jnp.where(keep, pltpu.roll(u, sh, axis=0), 0.0)
        u = a * u_s + u
        a = a * a_s
        sh *= 2
    h = u + a * hc_ref[...]
    hc_ref[...] = h[ts - 1:ts, :]

    y = h * jax.nn.gelu(gr_ref[0])
    y_ref[0] = _rms(y, gn_ref[...]).astype(BF16)


def _rglru(xr, gr, conv_w, conv_b, w_gate, b_gate, lam, gn, *, ts):
    bsz, seq, width = xr.shape
    body = functools.partial(_rglru_body, ts=ts, width=width)
    tok = lambda b, s: (b, s, 0)
    const = lambda b, s: (0, 0)
    return pl.pallas_call(
        body,
        grid=(bsz, seq // ts),
        in_specs=[
            pl.BlockSpec((1, ts, width), tok),
            pl.BlockSpec((1, ts, width), tok),
            pl.BlockSpec(conv_w.shape, const),
            pl.BlockSpec((1, width), const),
            pl.BlockSpec(w_gate.shape, const),
            pl.BlockSpec((1, 2 * width), const),
            pl.BlockSpec((1, width), const),
            pl.BlockSpec((1, width), const),
        ],
        out_specs=pl.BlockSpec((1, ts, width), tok),
        out_shape=jax.ShapeDtypeStruct((bsz, seq, width), BF16),
        scratch_shapes=[pltpu.VMEM((ts + SUBLANES, width), F32),
                        pltpu.VMEM((1, width), F32)],
        compiler_params=pltpu.CompilerParams(
            dimension_semantics=("parallel", "arbitrary"),
            vmem_limit_bytes=VMEM_LIMIT),
        name="rglru",
    )(xr, gr, conv_w, conv_b, w_gate, b_gate, lam, gn)


_ST_M, _ST_L = 0, 2
_PB_ALPHA, _PB_LFIN = 0, 4


def _fox_body(qi_off_ref, kj_off_ref, qi_diag_ref, qt_ref, k_ref, vt_ref, fq_ref, fk_ref,
              o_ref, s0_ref, s1_ref, p0_ref, p1_ref, p2_ref, p3_ref, st_ref, pbo_ref, pbd_ref,
              acc_ref, mb_ref,
              *, n_off, n_diag, tq, tk, dh):
    row_t = lax.broadcasted_iota(jnp.int32, (tk, tq), 0)
    col_t = lax.broadcasted_iota(jnp.int32, (tk, tq), 1)
    mb_ref[...] = jnp.where(row_t <= col_t, 0.0, NEG_INF)
    st_ref[:, _ST_M:_ST_M + 2, :] = jnp.full((st_ref.shape[0], 2, tq), NEG_INF, F32)
    st_ref[:, _ST_L:_ST_L + 2, :] = jnp.zeros((st_ref.shape[0], 2, tq), F32)
    acc_ref[...] = jnp.zeros_like(acc_ref)
    head_row = lax.broadcasted_iota(jnp.int32, (2 * dh, 1), 0)

    def stage_a(qi, kj, s_ref):
        qt = qt_ref[0, qi]
        kb = k_ref[0, pl.ds(pl.multiple_of(kj * tk, tk), tk), :]
        for h in range(2):
            in_head = (head_row >= h * dh) & (head_row < (h + 1) * dh)
            qm = jnp.where(in_head, qt, jnp.zeros_like(qt))
            s_ref[h] = jnp.dot(kb, qm, preferred_element_type=F32)

    def stage_b(qi, kj, s_ref, p_ref, pb_ref, buf, diagonal):
        fq2 = fq_ref[0, 0, qi]
        for h in range(2):
            fk = fk_ref[0, 0, pl.ds(pl.multiple_of(kj * tk, tk), tk), h:h + 1]
            fq = fq2[h:h + 1, :]
            z = s_ref[h] - fk
            if diagonal:
                z = z + mb_ref[...]
            mz = jnp.max(z, axis=0, keepdims=True) + fq
            m_prev = st_ref[qi, _ST_M + h:_ST_M + h + 1, :]
            m_new = jnp.maximum(m_prev, mz)
            alpha = jnp.exp2(m_prev - m_new)
            p = jnp.exp2(z - (m_new - fq))
            l_new = (alpha * st_ref[qi, _ST_L + h:_ST_L + h + 1, :]
                     + jnp.sum(p, axis=0, keepdims=True))
            p_ref[h] = p.astype(BF16)
            st_ref[qi, _ST_M + h:_ST_M + h + 1, :] = m_new
            st_ref[qi, _ST_L + h:_ST_L + h + 1, :] = l_new
            ra = _PB_ALPHA + 2 * buf + h
            pb_ref[ra:ra + 1, :] = alpha
            rl = _PB_LFIN + 2 * buf + h
            pb_ref[rl:rl + 1, :] = l_new

    def stage_c(qi, kj, p_ref, pb_ref, buf, diagonal):
        vt = vt_ref[0, kj]
        for h in range(2):
            rows = slice(h * dh, (h + 1) * dh)
            pv = jnp.dot(vt[rows, :], p_ref[h], preferred_element_type=F32)
            ra = _PB_ALPHA + 2 * buf + h
            acc = pb_ref[ra:ra + 1, :] * acc_ref[qi, rows, :] + pv
            acc_ref[qi, rows, :] = acc
            if diagonal:
                rl = _PB_LFIN + 2 * buf + h
                o_ref[0, qi, rows, :] = acc / pb_ref[rl:rl + 1, :]

    def run(n_items, item, diagonal, pa_ref, pz_ref, pb_ref):
        stage_a(*item(0), s0_ref)

        def pair(jj, carry):
            j0 = 2 * jj
            stage_a(*item(j0 + 1), s1_ref)
            stage_b(*item(j0), s0_ref, pa_ref, pb_ref, 0, diagonal)
            stage_c(*item(jnp.maximum(j0 - 1, 0)), pz_ref, pb_ref, 1, diagonal)
            stage_a(*item(jnp.minimum(j0 + 2, n_items - 1)), s0_ref)
            stage_b(*item(j0 + 1), s1_ref, pz_ref, pb_ref, 1, diagonal)
            stage_c(*item(j0), pa_ref, pb_ref, 0, diagonal)
            return carry

        lax.fori_loop(0, n_items // 2, pair, 0)
        stage_c(*item(n_items - 1), pz_ref, pb_ref, 1, diagonal)

    p1_ref[...] = jnp.zeros_like(p1_ref)
    p3_ref[...] = jnp.zeros_like(p3_ref)
    pbo_ref[...] = jnp.ones_like(pbo_ref)
    pbd_ref[...] = jnp.ones_like(pbd_ref)
    run(n_off, lambda j: (qi_off_ref[j], kj_off_ref[j]), False, p0_ref, p1_ref, pbo_ref)
    run(n_diag, lambda j: (qi_diag_ref[j], qi_diag_ref[j]), True, p2_ref, p3_ref, pbd_ref)


def _fox(qt, k, vt, f_row, f_col):
    bsz, nq, width, tq = qt.shape
    seq = k.shape[1]
    dh = width // FOX_HEADS
    pairs = FOX_HEADS // 2
    qi_off = np.concatenate([np.full(i, i, np.int32) for i in range(1, nq)])
    kj_off = np.concatenate([np.arange(i, dtype=np.int32) for i in range(1, nq)])
    qi_diag = np.arange(nq, dtype=np.int32)
    assert len(qi_off) % 2 == 0 and nq % 2 == 0, "the pipelined loops take two items per trip"
    body = functools.partial(_fox_body, n_off=len(qi_off), n_diag=nq, tq=tq, tk=tq, dh=dh)
    blk_t = pl.BlockSpec((1, nq, 2 * dh, tq), lambda b, p, *_: (b, 0, p, 0))
    grid_spec = pltpu.PrefetchScalarGridSpec(
        num_scalar_prefetch=3,
        grid=(bsz, pairs),
        in_specs=[
            blk_t,
            pl.BlockSpec((1, seq, 2 * dh), lambda b, p, *_: (b, 0, p)),
            blk_t,
            pl.BlockSpec((1, 1, nq, 2, tq), lambda b, p, *_: (b, p, 0, 0, 0)),
            pl.BlockSpec((1, 1, seq, 2), lambda b, p, *_: (b, p, 0, 0)),
        ],
        out_specs=blk_t,
        scratch_shapes=[pltpu.VMEM((2, tq, tq), F32),
                        pltpu.VMEM((2, tq, tq), F32),
                        pltpu.VMEM((2, tq, tq), BF16),
                        pltpu.VMEM((2, tq, tq), BF16),
                        pltpu.VMEM((2, tq, tq), BF16),
                        pltpu.VMEM((2, tq, tq), BF16),
                        pltpu.VMEM((nq, SUBLANES, tq), F32),
                        pltpu.VMEM((SUBLANES, tq), F32),
                        pltpu.VMEM((SUBLANES, tq), F32),
                        pltpu.VMEM((nq, 2 * dh, tq), F32),
                        pltpu.VMEM((tq, tq), F32)],
    )
    return pl.pallas_call(
        body,
        grid_spec=grid_spec,
        out_shape=jax.ShapeDtypeStruct((bsz, nq, width, tq), F32),
        compiler_params=pltpu.CompilerParams(
            dimension_semantics=("parallel", "parallel"),
            vmem_limit_bytes=VMEM_LIMIT),
        name="fox",
    )(jnp.asarray(qi_off), jnp.asarray(kj_off), jnp.asarray(qi_diag), qt, k, vt, f_row, f_col)


def _mid_body(x_ref, yl_ref, yf_ref, gnf_ref, wo_ref, g2_ref, wqt_ref, sk_ref,
              x1_ref, h2t_ref, s1_ref, s2_ref, *, half):
    yfn = _rms(yf_ref[0, 0].T, gnf_ref[...]).astype(BF16)
    mix = jnp.dot(yl_ref[...], wo_ref[0:half, :], preferred_element_type=F32)
    mix = mix + jnp.dot(yfn, wo_ref[half:2 * half, :], preferred_element_type=F32)
    x1 = x_ref[...] + mix
    x1_ref[...] = x1
    h2t = _rms(x1, g2_ref[...]).T.astype(BF16)
    h2t_ref[...] = h2t
    qpt = jnp.dot(wqt_ref[...], h2t, preferred_element_type=F32)
    for hp in range(2 * PEER_HEADS):
        qs = qpt[hp * LANES:(hp + 1) * LANES, :].astype(BF16)
        sc = jnp.dot(sk_ref[hp], qs, preferred_element_type=F32)
        if hp % 2 == 0:
            s1_ref[hp // 2] = sc
        else:
            s2_ref[hp // 2] = sc


def _mid(x2d, y_lru, y_fox_t, gn_fox, w_out, g2, wq_t, subkeys, *, tm):
    t, d = x2d.shape
    half = y_lru.shape[1]
    _, n_blk, _, t_blk = y_fox_t.shape
    per_blk = t_blk // tm
    body = functools.partial(_mid_body, half=half)
    tok = lambda i: (i, 0)
    const2 = lambda i: (0, 0)
    sc_shape = jax.ShapeDtypeStruct((PEER_HEADS, PEER_NKEYS, t), F32)
    sc_spec = pl.BlockSpec((PEER_HEADS, PEER_NKEYS, tm), lambda i: (0, 0, i))
    return pl.pallas_call(
        body,
        grid=(t // tm,),
        in_specs=[
            pl.BlockSpec((tm, d), tok),
            pl.BlockSpec((tm, half), tok),
            pl.BlockSpec((1, 1, half, tm),
                         lambda i: (i // (per_blk * n_blk), (i // per_blk) % n_blk, 0,
                                    i % per_blk)),
            pl.BlockSpec((1, half), const2),
            pl.BlockSpec(w_out.shape, const2),
            pl.BlockSpec((1, d), const2),
            pl.BlockSpec(wq_t.shape, const2),
            pl.BlockSpec(subkeys.shape, lambda i: (0, 0, 0)),
        ],
        out_specs=[pl.BlockSpec((tm, d), tok),
                   pl.BlockSpec((d, tm), lambda i: (0, i)),
                   sc_spec, sc_spec],
        out_shape=[jax.ShapeDtypeStruct((t, d), F32),
                   jax.ShapeDtypeStruct((d, t), BF16),
                   sc_shape, sc_shape],
        compiler_params=pltpu.CompilerParams(
            dimension_semantics=("parallel",),
            vmem_limit_bytes=VMEM_LIMIT),
        name="mid",
    )(x2d, y_lru, y_fox_t, gn_fox, w_out, g2, wq_t, subkeys)


_CAND = [(a, b) for a in range(PEER_TOPK) for b in range(PEER_TOPK)
         if (a + 1) * (b + 1) <= PEER_TOPK]
_CAND_ROWS = -(-len(_CAND) // SUBLANES) * SUBLANES


def _top_values(x, n, want_rank=False):
    vals = []
    rank = jnp.full(x.shape, float(n), F32) if want_rank else None
    for r in range(n):
        mx = jnp.max(x, axis=0, keepdims=True)
        vals.append(mx)
        hit = x == mx
        x = jnp.where(hit, NEG_INF, x)
        if want_rank:
            rank = jnp.where(hit, float(r), rank)
    return vals, rank


def _gates_body(s1_ref, s2_ref, n_ref, p1_ref, rank_ref, p2_ref, cand_ref):
    s1 = s1_ref[0]
    s2 = s2_ref[0]
    a, _ = _top_values(s1, PEER_TOPK)
    b, rank = _top_values(s2, PEER_TOPK, want_rank=True)
    cand_ref[...] = jnp.full_like(cand_ref, NEG_INF)
    sums = [a[ia] + b[ib] for ia, ib in _CAND]
    for r, v in enumerate(sums):
        cand_ref[r:r + 1, :] = v
    c, _ = _top_values(cand_ref[...], PEER_TOPK)
    z = jnp.zeros_like(c[0])
    for r in range(PEER_TOPK):
        z = z + jnp.exp(c[r] - c[0])
    tau = c[PEER_TOPK - 1]
    picked = [jnp.zeros_like(tau) for _ in range(PEER_TOPK)]
    for (ia, _), v in zip(_CAND, sums):
        picked[ia] = picked[ia] + jnp.where(v >= tau, 1.0, 0.0)
    n = jnp.zeros_like(s1)
    for ia in range(PEER_TOPK):
        n = jnp.where(s1 == a[ia], picked[ia], n)
    n_ref[0] = n
    p1_ref[0] = 0.5 * jnp.exp(s1 - a[0]) / z
    rank_ref[0] = rank.astype(BF16)
    p2_ref[0] = jnp.exp(s2 - b[0]).astype(BF16)


def _gates(s1, s2, *, tm):
    heads, nk, t = s1.shape
    spec = pl.BlockSpec((1, nk, tm), lambda i, h: (h, 0, i))
    shape = jax.ShapeDtypeStruct(s1.shape, F32)
    shape_b = jax.ShapeDtypeStruct(s1.shape, BF16)
    return pl.pallas_call(
        _gates_body,
        grid=(t // tm, heads),
        in_specs=[spec, spec],
        out_specs=[spec, spec, spec, spec],
        out_shape=[shape, shape, shape_b, shape_b],
        scratch_shapes=[pltpu.VMEM((_CAND_ROWS, tm), F32)],
        compiler_params=pltpu.CompilerParams(
            dimension_semantics=("parallel", "parallel"),
            vmem_limit_bytes=VMEM_LIMIT),
        name="gates",
    )(s1, s2)


def _peer_body(h2_ref, u_ref, vt_ref, rank_ref, p2_ref, n_ref, p1_ref, x1_ref, gf_ref,
               o_ref, pre_ref, acc_ref, *, per_tile, rows_per_item):
    s = pl.program_id(0)
    n = s - 1

    @pl.when(s == 0)
    def _():
        pre_ref[...] = jnp.zeros_like(pre_ref)
        acc_ref[...] = jnp.zeros_like(acc_ref)

    pre_next = jnp.dot(u_ref[...], h2_ref[...], preferred_element_type=F32)

    tm = pre_ref.shape[1]

    def packed_row(ref, h, r):
        tile = jnp.broadcast_to(ref[h, r:r + 1, :], (BF16_ROWS, tm)).astype(BF16)
        return tile[None]

    chunks = []
    groups = PEER_NKEYS // BF16_ROWS
    for r in range(rows_per_item):
        w = jnp.zeros((groups, BF16_ROWS, tm), BF16)
        for h in range(PEER_HEADS):
            rank = rank_ref[h].reshape(groups, BF16_ROWS, tm)
            p2 = p2_ref[h].reshape(groups, BF16_ROWS, tm)
            sel = jnp.where(rank < packed_row(n_ref, h, r), p2, jnp.zeros_like(p2))
            w = w + sel * packed_row(p1_ref, h, r)
        rows = slice(r * PEER_NKEYS, (r + 1) * PEER_NKEYS)
        xb = pre_ref[rows, :].astype(BF16)
        tb = _gelu_tanh_factor(xb)
        chunks.append((xb + xb * tb) * w.reshape(PEER_NKEYS, tm))
    wact = jnp.concatenate(chunks, axis=0)

    pv = jnp.dot(vt_ref[...], wact, preferred_element_type=F32)
    first = (n % per_tile) == 0
    acc_ref[...] = jnp.where(first, pv, acc_ref[...] + pv)
    pre_ref[...] = pre_next

    @pl.when((n >= 0) & (n % per_tile == per_tile - 1))
    def _():
        xo = x1_ref[...] + acc_ref[...].T
        o_ref[...] = _rms(xo, gf_ref[...])


def _peer(h2t, u, vt, rank, p2, n_word, p1_word, x1, gf, *, tm, te):
    d, t = h2t.shape
    n_exp = u.shape[0]
    per_tile = n_exp // te
    rows_per_item = te // PEER_NKEYS
    n_items = (t // tm) * per_tile
    body = functools.partial(_peer_body, per_tile=per_tile, rows_per_item=rows_per_item)

    def item(s, off):
        return jnp.clip(s - 1 + off, 0, n_items - 1)

    tile = lambda s, off: item(s, off) // per_tile
    blk = lambda s, off: item(s, off) % per_tile
    full_spec = pl.BlockSpec((PEER_HEADS, PEER_NKEYS, tm), lambda s: (0, 0, tile(s, 0)))
    rows_spec = pl.BlockSpec((PEER_HEADS, rows_per_item, tm),
                             lambda s: (0, blk(s, 0), tile(s, 0)))
    tok_spec = pl.BlockSpec((tm, d), lambda s: (tile(s, 0), 0))
    return pl.pallas_call(
        body,
        grid=(n_items + 1,),
        in_specs=[
            pl.BlockSpec((d, tm), lambda s: (0, tile(s, 1))),
            pl.BlockSpec((te, d), lambda s: (blk(s, 1), 0)),
            pl.BlockSpec((d, te), lambda s: (0, blk(s, 0))),
            full_spec, full_spec, rows_spec, rows_spec, tok_spec,
            pl.BlockSpec((1, d), lambda s: (0, 0)),
        ],
        out_specs=tok_spec,
        out_shape=jax.ShapeDtypeStruct((t, d), F32),
        scratch_shapes=[pltpu.VMEM((te, tm), F32), pltpu.VMEM((d, tm), F32)],
        compiler_params=pltpu.CompilerParams(
            dimension_semantics=("arbitrary",),
            vmem_limit_bytes=VMEM_LIMIT),
        name="peer",
    )(h2t, u, vt, rank, p2, n_word, p1_word, x1, gf)


def _block_diag(w):
    heads, dh, _ = w.shape
    eye = jnp.eye(heads, dtype=w.dtype)
    return jnp.einsum("hij,hg->higj", w, eye).reshape(heads * dh, heads * dh)


def _layer(x, norm1_g, w_in, conv_w, conv_b, lru_wa, lru_ba, lru_wx, lru_bx, lru_lambda,
           fox_bf, gn_lru_g, gn_fox_g, w_out, norm2_g, peer_wq, peer_subkeys, peer_u,
           peer_v, out_g):
    bsz, seq, d = x.shape
    width = lru_lambda.shape[0]
    row = lambda v: v.reshape(1, -1).astype(F32)

    w_main = w_in[:, :5 * width].astype(BF16)
    w_fl = jnp.pad(w_in[:, 5 * width:], ((0, 0), (0, LANES - FOX_HEADS))).astype(BF16)
    b_fl = jnp.pad(fox_bf.astype(F32), (0, LANES - FOX_HEADS)).reshape(1, LANES)
    t_blk = FOX_BLOCK
    xr, gr, qt, k, vt, f_t = _inproj(x, row(norm1_g), w_main, w_fl, b_fl, tm=t_blk)

    w_gate = jnp.concatenate([_block_diag(lru_wa), _block_diag(lru_wx)], axis=1).astype(BF16)
    b_gate = jnp.concatenate([lru_ba, lru_bx]).reshape(1, -1).astype(F32)
    y_lru = _rglru(xr, gr, conv_w.astype(F32), row(conv_b), w_gate, b_gate,
                   row(lru_lambda), row(gn_lru_g), ts=256)

    pairs = FOX_HEADS // 2
    f_pair = f_t.reshape(bsz, pairs, 2, seq)
    f_col = f_pair.transpose(0, 1, 3, 2)
    f_row = f_pair.reshape(bsz, pairs, 2, seq // t_blk, t_blk).transpose(0, 1, 3, 2, 4)
    y_fox_t = _fox(qt, k, vt, f_row, f_col)

    t = bsz * seq
    n_hp = 2 * PEER_HEADS
    subkeys = peer_subkeys.reshape(n_hp, PEER_NKEYS, -1).astype(BF16)
    x1, h2t, s1, s2 = _mid(x.reshape(t, d), y_lru.reshape(t, width), y_fox_t,
                           row(gn_fox_g), w_out.astype(BF16), row(norm2_g),
                           peer_wq.T.astype(BF16), subkeys, tm=512)

    n_sel, p1, rank2, p2 = _gates(s1, s2, tm=256)
    out = _peer(h2t, peer_u.astype(BF16), peer_v.T.astype(BF16), rank2, p2, n_sel, p1, x1,
                row(out_g), tm=512, te=2048)
    return out.reshape(bsz, seq, d)


def kernel(x, norm1_g, w_in, conv_w, conv_b, lru_wa, lru_ba, lru_wx, lru_bx, lru_lambda,
           fox_bf, gn_lru_g, gn_fox_g, w_out, norm2_g, peer_wq, peer_subkeys, peer_u,
           peer_v, final_g):
    depth = w_in.shape[0]
    assert depth == 1, "the fused final RMSNorm assumes a single trunk layer"
    return _layer(x, norm1_g[0], w_in[0], conv_w[0], conv_b[0], lru_wa[0], lru_ba[0],
                  lru_wx[0], lru_bx[0], lru_lambda[0], fox_bf[0], gn_lru_g[0], gn_fox_g[0],
                  w_out[0], norm2_g[0], peer_wq[0], peer_subkeys[0], peer_u[0], peer_v[0],
                  final_g)
```

```python
import functools

import jax
import jax.numpy as jnp
import numpy as np
from jax import lax
from jax.experimental import pallas as pl
from jax.experimental.pallas import tpu as pltpu

F32 = jnp.float32
BF16 = jnp.bfloat16

RMS_EPS = 1e-6
LRU_C = 8.0
CONV_WIDTH = 4
LRU_HEADS = 8
FOX_HEADS = 8
PEER_HEADS = 8
PEER_NKEYS = 128
PEER_TOPK = 16
FOX_BLOCK = 512

LANES = 128
SUBLANES = 8
VMEM_LIMIT = 56 * 1024 * 1024

NEG_INF = float("-inf")
LOG2E = 1.4426950408889634
_GELU_K = 0.7978845608028654
BF16_ROWS = 16


def _rms(x, g):
    return x * lax.rsqrt(jnp.mean(x * x, axis=-1, keepdims=True) + RMS_EPS) * g


def _gelu_tanh_factor(x):
    return jnp.tanh(x * (_GELU_K + (_GELU_K * 0.044715) * (x * x)))


def _inproj_body(x_ref, g_ref, w_ref, wfl_ref, bf_ref,
                 xr_ref, gr_ref, qt_ref, k_ref, vt_ref, ft_ref, carry_ref,
                 *, tm, width, q_scale):
    s = pl.program_id(1)

    @pl.when(s == 0)
    def _():
        carry_ref[...] = jnp.zeros_like(carry_ref)

    hb = _rms(x_ref[0], g_ref[...]).astype(BF16)

    def piece(i):
        return jnp.dot(hb, w_ref[:, i * width:(i + 1) * width],
                       preferred_element_type=F32)

    xr_ref[0] = piece(0)
    gr_ref[0] = piece(1)
    qt_ref[0, 0] = (piece(2) * (q_scale * LOG2E)).T.astype(BF16)
    k_ref[0] = piece(3).astype(BF16)
    vt_ref[0, 0] = piece(4).T.astype(BF16)

    fl = jnp.dot(hb, wfl_ref[...], preferred_element_type=F32)
    c = jax.nn.log_sigmoid(fl + bf_ref[...]) * LOG2E
    row = lax.broadcasted_iota(jnp.int32, c.shape, 0)
    sh = 1
    while sh < tm:
        c = c + jnp.where(row >= sh, pltpu.roll(c, sh, axis=0), 0.0)
        sh *= 2
    f = c + carry_ref[...]
    carry_ref[...] = f[tm - 1:tm, :]
    ft_ref[0] = f.T[0:FOX_HEADS, :]


def _inproj(x, g, w_main, w_fl, b_fl, *, tm):
    bsz, seq, d = x.shape
    width = w_main.shape[1] // 5
    body = functools.partial(_inproj_body, tm=tm, width=width,
                             q_scale=(width // FOX_HEADS) ** -0.5)
    tok = lambda b, s: (b, s, 0)
    const = lambda b, s: (0, 0)
    big = jax.ShapeDtypeStruct((bsz, seq, width), F32)
    bigb = jax.ShapeDtypeStruct((bsz, seq, width), BF16)
    blocked_t = jax.ShapeDtypeStruct((bsz, seq // tm, width, tm), BF16)
    return pl.pallas_call(
        body,
        grid=(bsz, seq // tm),
        in_specs=[
            pl.BlockSpec((1, tm, d), tok),
            pl.BlockSpec((1, d), const),
            pl.BlockSpec(w_main.shape, const),
            pl.BlockSpec(w_fl.shape, const),
            pl.BlockSpec((1, LANES), const),
        ],
        out_specs=[pl.BlockSpec((1, tm, width), tok),
                   pl.BlockSpec((1, tm, width), tok),
                   pl.BlockSpec((1, 1, width, tm), lambda b, s: (b, s, 0, 0)),
                   pl.BlockSpec((1, tm, width), tok),
                   pl.BlockSpec((1, 1, width, tm), lambda b, s: (b, s, 0, 0)),
                   pl.BlockSpec((1, FOX_HEADS, tm), lambda b, s: (b, 0, s))],
        out_shape=[big, big, blocked_t, bigb, blocked_t,
                   jax.ShapeDtypeStruct((bsz, FOX_HEADS, seq), F32)],
        scratch_shapes=[pltpu.VMEM((1, LANES), F32)],
        compiler_params=pltpu.CompilerParams(
            dimension_semantics=("parallel", "arbitrary"),
            vmem_limit_bytes=VMEM_LIMIT),
        name="inproj",
    )(x, g, w_main, w_fl, b_fl)


def _rglru_body(xr_ref, gr_ref, cw_ref, cb_ref, wg_ref, bg_ref, lam_ref, gn_ref,
                y_ref, xbuf_ref, hc_ref, *, ts, width):
    s = pl.program_id(1)

    @pl.when(s == 0)
    def _():
        xbuf_ref[0:SUBLANES, :] = jnp.zeros((SUBLANES, width), F32)
        hc_ref[...] = jnp.zeros_like(hc_ref)

    x = xr_ref[0]
    xbuf_ref[SUBLANES:SUBLANES + ts, :] = x
    cw = cw_ref[...]
    conv = cb_ref[...] + cw[CONV_WIDTH - 1:CONV_WIDTH, :] * x
    for back in range(1, CONV_WIDTH):
        k = CONV_WIDTH - 1 - back
        conv = conv + cw[k:k + 1, :] * xbuf_ref[SUBLANES - back:SUBLANES - back + ts, :]
    xbuf_ref[0:SUBLANES, :] = x[ts - SUBLANES:ts, :]

    gates = jnp.dot(conv.astype(BF16), wg_ref[...],
                    preferred_element_type=F32) + bg_ref[...]
    r = jax.nn.sigmoid(gates[:, :width])
    i = jax.nn.sigmoid(gates[:, width:])
    log_a = (-LRU_C) * r * jax.nn.softplus(-lam_ref[...])
    a = jnp.exp(log_a)
    u = jnp.sqrt(1.0 - a * a) * (i * conv)

    row = lax.broadcasted_iota(jnp.int32, a.shape, 0)
    sh = 1
    while sh < ts:
        keep = row >= sh
        a_s = jnp.where(keep, pltpu.roll(a, sh, axis=0), 1.0)
        u_s = jnp.where(keep, pltpu.roll(u, sh, axis=0), 0.0)
        u = a * u_s + u
        a = a * a_s
        sh *= 2
    h = u + a * hc_ref[...]
    hc_ref[...] = h[ts - 1:ts, :]

    y = h * jax.nn.gelu(gr_ref[0])
    y_ref[0] = _rms(y, gn_ref[...]).astype(BF16)


def _rglru(xr, gr, conv_w, conv_b, w_gate, b_gate, lam, gn, *, ts):
    bsz, seq, width = xr.shape
    body = functools.partial(_rglru_body, ts=ts, width=width)
    tok = lambda b, s: (b, s, 0)
    const = lambda b, s: (0, 0)
    return pl.pallas_call(
        body,
        grid=(bsz, seq // ts),
        in_specs=[
            pl.BlockSpec((1, ts, width), tok),
            pl.BlockSpec((1, ts, width), tok),
            pl.BlockSpec(conv_w.shape, const),
            pl.BlockSpec((1, width), const),
            pl.BlockSpec(w_gate.shape, const),
            pl.BlockSpec((1, 2 * width), const),
            pl.BlockSpec((1, width), const),
            pl.BlockSpec((1, width), const),
        ],
        out_specs=pl.BlockSpec((1, ts, width), tok),
        out_shape=jax.ShapeDtypeStruct((bsz, seq, width), BF16),
        scratch_shapes=[pltpu.VMEM((ts + SUBLANES, width), F32),
                        pltpu.VMEM((1, width), F32)],
        compiler_params=pltpu.CompilerParams(
            dimension_semantics=("parallel", "arbitrary"),
            vmem_limit_bytes=VMEM_LIMIT),
        name="rglru",
    )(xr, gr, conv_w, conv_b, w_gate, b_gate, lam, gn)


_ST_M, _ST_L = 0, 2
_PB_ALPHA, _PB_LFIN = 0, 4


def _fox_body(qi_off_ref, kj_off_ref, qi_diag_ref, qt_ref, k_ref, vt_ref, fq_ref, fk_ref,
              o_ref, s0_ref, s1_ref, p0_ref, p1_ref, p2_ref, p3_ref, st_ref, pbo_ref, pbd_ref,
              acc_ref, mb_ref,
              *, n_off, n_diag, tq, tk, dh):
    row_t = lax.broadcasted_iota(jnp.int32, (tk, tq), 0)
    col_t = lax.broadcasted_iota(jnp.int32, (tk, tq), 1)
    mb_ref[...] = jnp.where(row_t <= col_t, 0.0, NEG_INF)
    st_ref[:, _ST_M:_ST_M + 2, :] = jnp.full((st_ref.shape[0], 2, tq), NEG_INF, F32)
    st_ref[:, _ST_L:_ST_L + 2, :] = jnp.zeros((st_ref.shape[0], 2, tq), F32)
    acc_ref[...] = jnp.zeros_like(acc_ref)
    head_row = lax.broadcasted_iota(jnp.int32, (2 * dh, 1), 0)

    def stage_a(qi, kj, s_ref):
        qt = qt_ref[0, qi]
        kb = k_ref[0, pl.ds(pl.multiple_of(kj * tk, tk), tk), :]
        for h in range(2):
            in_head = (head_row >= h * dh) & (head_row < (h + 1) * dh)
            qm = jnp.where(in_head, qt, jnp.zeros_like(qt))
            s_ref[h] = jnp.dot(kb, qm, preferred_element_type=F32)

    def stage_b(qi, kj, s_ref, p_ref, pb_ref, buf, diagonal):
        fq2 = fq_ref[0, 0, qi]
        for h in range(2):
            fk = fk_ref[0, 0, pl.ds(pl.multiple_of(kj * tk, tk), tk), h:h + 1]
            fq = fq2[h:h + 1, :]
            z = s_ref[h] - fk
            if diagonal:
                z = z + mb_ref[...]
            mz = jnp.max(z, axis=0, keepdims=True) + fq
            m_prev = st_ref[qi, _ST_M + h:_ST_M + h + 1, :]
            m_new = jnp.maximum(m_prev, mz)
            alpha = jnp.exp2(m_prev - m_new)
            p = jnp.exp2(z - (m_new - fq))
            l_new = (alpha * st_ref[qi, _ST_L + h:_ST_L + h + 1, :]
                     + jnp.sum(p, axis=0, keepdims=True))
            p_ref[h] = p.astype(BF16)
            st_ref[qi, _ST_M + h:_ST_M + h + 1, :] = m_new
            st_ref[qi, _ST_L + h:_ST_L + h + 1, :] = l_new
            ra = _PB_ALPHA + 2 * buf + h
            pb_ref[ra:ra + 1, :] = alpha
            rl = _PB_LFIN + 2 * buf + h
            pb_ref[rl:rl + 1, :] = l_new

    def stage_c(qi, kj, p_ref, pb_ref, buf, diagonal):
        vt = vt_ref[0, kj]
        for h in range(2):
            rows = slice(h * dh, (h + 1) * dh)
            pv = jnp.dot(vt[rows, :], p_ref[h], preferred_element_type=F32)
            ra = _PB_ALPHA + 2 * buf + h
            acc = pb_ref[ra:ra + 1, :] * acc_ref[qi, rows, :] + pv
            acc_ref[qi, rows, :] = acc
            if diagonal:
                rl = _PB_LFIN + 2 * buf + h
                o_ref[0, qi, rows, :] = acc / pb_ref[rl:rl + 1, :]

    def run(n_items, item, diagonal, pa_ref, pz_ref, pb_ref):
        stage_a(*item(0), s0_ref)

        def pair(jj, carry):
            j0 = 2 * jj
            stage_a(*item(j0 + 1), s1_ref)
            stage_b(*item(j0), s0_ref, pa_ref, pb_ref, 0, diagonal)
            stage_c(*item(jnp.maximum(j0 - 1, 0)), pz_ref, pb_ref, 1, diagonal)
            stage_a(*item(jnp.minimum(j0 + 2, n_items - 1)), s0_ref)
            stage_b(*item(j0 + 1), s1_ref, pz_ref, pb_ref, 1, diagonal)
            stage_c(*item(j0), pa_ref, pb_ref, 0, diagonal)
            return carry

        lax.fori_loop(0, n_items // 2, pair, 0)
        stage_c(*item(n_items - 1), pz_ref, pb_ref, 1, diagonal)

    p1_ref[...] = jnp.zeros_like(p1_ref)
    p3_ref[...] = jnp.zeros_like(p3_ref)
    pbo_ref[...] = jnp.ones_like(pbo_ref)
    pbd_ref[...] = jnp.ones_like(pbd_ref)
    run(n_off, lambda j: (qi_off_ref[j], kj_off_ref[j]), False, p0_ref, p1_ref, pbo_ref)
    run(n_diag, lambda j: (qi_diag_ref[j], qi_diag_ref[j]), True, p2_ref, p3_ref, pbd_ref)


def _fox(qt, k, vt, f_row, f_col):
    bsz, nq, width, tq = qt.shape
    seq = k.shape[1]
    dh = width // FOX_HEADS
    pairs = FOX_HEADS // 2
    qi_off = np.concatenate([np.full(i, i, np.int32) for i in range(1, nq)])
    kj_off = np.concatenate([np.arange(i, dtype=np.int32) for i in range(1, nq)])
    qi_diag = np.arange(nq, dtype=np.int32)
    assert len(qi_off) % 2 == 0 and nq % 2 == 0, "the pipelined loops take two items per trip"
    body = functools.partial(_fox_body, n_off=len(qi_off), n_diag=nq, tq=tq, tk=tq, dh=dh)
    blk_t = pl.BlockSpec((1, nq, 2 * dh, tq), lambda b, p, *_: (b, 0, p, 0))
    grid_spec = pltpu.PrefetchScalarGridSpec(
        num_scalar_prefetch=3,
        grid=(bsz, pairs),
        in_specs=[
            blk_t,
            pl.BlockSpec((1, seq, 2 * dh), lambda b, p, *_: (b, 0, p)),
            blk_t,
            pl.BlockSpec((1, 1, nq, 2, tq), lambda b, p, *_: (b, p, 0, 0, 0)),
            pl.BlockSpec((1, 1, seq, 2), lambda b, p, *_: (b, p, 0, 0)),
        ],
        out_specs=blk_t,
        scratch_shapes=[pltpu.VMEM((2, tq, tq), F32),
                        pltpu.VMEM((2, tq, tq), F32),
                        pltpu.VMEM((2, tq, tq), BF16),
                        pltpu.VMEM((2, tq, tq), BF16),
                        pltpu.VMEM((2, tq, tq), BF16),
                        pltpu.VMEM((2, tq, tq), BF16),
                        pltpu.VMEM((nq, SUBLANES, tq), F32),
                        pltpu.VMEM((SUBLANES, tq), F32),
                        pltpu.VMEM((SUBLANES, tq), F32),
                        pltpu.VMEM((nq, 2 * dh, tq), F32),
                        pltpu.VMEM((tq, tq), F32)],
    )
    return pl.pallas_call(
        body,
        grid_spec=grid_spec,
        out_shape=jax.ShapeDtypeStruct((bsz, nq, width, tq), F32),
        compiler_params=pltpu.CompilerParams(
            dimension_semantics=("parallel", "parallel"),
            vmem_limit_bytes=VMEM_LIMIT),
        name="fox",
    )(jnp.asarray(qi_off), jnp.asarray(kj_off), jnp.asarray(qi_diag), qt, k, vt, f_row, f_col)


def _mid_body(x_ref, yl_ref, yf_ref, gnf_ref, wo_ref, g2_ref, wqt_ref, sk_ref,
              x1_ref, h2t_ref, s1_ref, s2_ref, *, half):
    yfn = _rms(yf_ref[0, 0].T, gnf_ref[...]).astype(BF16)
    mix = jnp.dot(yl_ref[...], wo_ref[0:half, :], preferred_element_type=F32)
    mix = mix + jnp.dot(yfn, wo_ref[half:2 * half, :], preferred_element_type=F32)
    x1 = x_ref[...] + mix
    x1_ref[...] = x1
    h2t = _rms(x1, g2_ref[...]).T.astype(BF16)
    h2t_ref[...] = h2t
    qpt = jnp.dot(wqt_ref[...], h2t, preferred_element_type=F32)
    for hp in range(2 * PEER_HEADS):
        qs = qpt[hp * LANES:(hp + 1) * LANES, :].astype(BF16)
        sc = jnp.dot(sk_ref[hp], qs, preferred_element_type=F32)
        if hp % 2 == 0:
            s1_ref[hp // 2] = sc
        else:
            s2_ref[hp // 2] = sc


def _mid(x2d, y_lru, y_fox_t, gn_fox, w_out, g2, wq_t, subkeys, *, tm):
    t, d = x2d.shape
    half = y_lru.shape[1]
    _, n_blk, _, t_blk = y_fox_t.shape
    per_blk = t_blk // tm
    body = functools.partial(_mid_body, half=half)
    tok = lambda i: (i, 0)
    const2 = lambda i: (0, 0)
    sc_shape = jax.ShapeDtypeStruct((PEER_HEADS, PEER_NKEYS, t), F32)
    sc_spec = pl.BlockSpec((PEER_HEADS, PEER_NKEYS, tm), lambda i: (0, 0, i))
    return pl.pallas_call(
        body,
        grid=(t // tm,),
        in_specs=[
            pl.BlockSpec((tm, d), tok),
            pl.BlockSpec((tm, half), tok),
            pl.BlockSpec((1, 1, half, tm),
                         lambda i: (i // (per_blk * n_blk), (i // per_blk) % n_blk, 0,
                                    i % per_blk)),
            pl.BlockSpec((1, half), const2),
            pl.BlockSpec(w_out.shape, const2),
            pl.BlockSpec((1, d), const2),
            pl.BlockSpec(wq_t.shape, const2),
            pl.BlockSpec(subkeys.shape, lambda i: (0, 0, 0)),
        ],
        out_specs=[pl.BlockSpec((tm, d), tok),
                   pl.BlockSpec((d, tm), lambda i: (0, i)),
                   sc_spec, sc_spec],
        out_shape=[jax.ShapeDtypeStruct((t, d), F32),
                   jax.ShapeDtypeStruct((d, t), BF16),
                   sc_shape, sc_shape],
        compiler_params=pltpu.CompilerParams(
            dimension_semantics=("parallel",),
            vmem_limit_bytes=VMEM_LIMIT),
        name="mid",
    )(x2d, y_lru, y_fox_t, gn_fox, w_out, g2, wq_t, subkeys)


_CAND = [(a, b) for a in range(PEER_TOPK) for b in range(PEER_TOPK)
         if (a + 1) * (b + 1) <= PEER_TOPK]
_CAND_ROWS = -(-len(_CAND) // SUBLANES) * SUBLANES


def _top_values(x, n, want_rank=False):
    vals = []
    rank = jnp.full(x.shape, float(n), F32) if want_rank else None
    for r in range(n):
        mx = jnp.max(x, axis=0, keepdims=True)
        vals.append(mx)
        hit = x == mx
        x = jnp.where(hit, NEG_INF, x)
        if want_rank:
            rank = jnp.where(hit, float(r), rank)
    return vals, rank


def _gates_body(s1_ref, s2_ref, n_ref, p1_ref, rank_ref, p2_ref, cand_ref):
    s1 = s1_ref[0]
    s2 = s2_ref[0]
    a, _ = _top_values(s1, PEER_TOPK)
    b, rank = _top_values(s2, PEER_TOPK, want_rank=True)
    cand_ref[...] = jnp.full_like(cand_ref, NEG_INF)
    sums = [a[ia] + b[ib] for ia, ib in _CAND]
    for r, v in enumerate(sums):
        cand_ref[r:r + 1, :] = v
    c, _ = _top_values(cand_ref[...], PEER_TOPK)
    z = jnp.zeros_like(c[0])
    for r in range(PEER_TOPK):
        z = z + jnp.exp(c[r] - c[0])
    tau = c[PEER_TOPK - 1]
    picked = [jnp.zeros_like(tau) for _ in range(PEER_TOPK)]
    for (ia, _), v in zip(_CAND, sums):
        picked[ia] = picked[ia] + jnp.where(v >= tau, 1.0, 0.0)
    n = jnp.zeros_like(s1)
    for ia in range(PEER_TOPK):
        n = jnp.where(s1 == a[ia], picked[ia], n)
    n_ref[0] = n
    p1_ref[0] = 0.5 * jnp.exp(s1 - a[0]) / z
    rank_ref[0] = rank.astype(BF16)
    p2_ref[0] = jnp.exp(s2 - b[0]).astype(BF16)


def _gates(s1, s2, *, tm):
    heads, nk, t = s1.shape
    spec = pl.BlockSpec((1, nk, tm), lambda i, h: (h, 0, i))
    shape = jax.ShapeDtypeStruct(s1.shape, F32)
    shape_b = jax.ShapeDtypeStruct(s1.shape, BF16)
    return pl.pallas_call(
        _gates_body,
        grid=(t // tm, heads),
        in_specs=[spec, spec],
        out_specs=[spec, spec, spec, spec],
        out_shape=[shape, shape, shape_b, shape_b],
        scratch_shapes=[pltpu.VMEM((_CAND_ROWS, tm), F32)],
        compiler_params=pltpu.CompilerParams(
            dimension_semantics=("parallel", "parallel"),
            vmem_limit_bytes=VMEM_LIMIT),
        name="gates",
    )(s1, s2)


def _peer_body(h2_ref, u_ref, vt_ref, rank_ref, p2_ref, n_ref, p1_ref, x1_ref, gf_ref,
               o_ref, pre_ref, acc_ref, *, per_tile, rows_per_item):
    s = pl.program_id(0)
    n = s - 1

    @pl.when(s == 0)
    def _():
        pre_ref[...] = jnp.zeros_like(pre_ref)
        acc_ref[...] = jnp.zeros_like(acc_ref)

    pre_next = jnp.dot(u_ref[...], h2_ref[...], preferred_element_type=F32)

    tm = pre_ref.shape[1]

    def packed_row(ref, h, r):
        tile = jnp.broadcast_to(ref[h, r:r + 1, :], (BF16_ROWS, tm)).astype(BF16)
        return tile[None]

    chunks = []
    groups = PEER_NKEYS // BF16_ROWS
    for r in range(rows_per_item):
        w = jnp.zeros((groups, BF16_ROWS, tm), BF16)
        for h in range(PEER_HEADS):
            rank = rank_ref[h].reshape(groups, BF16_ROWS, tm)
            p2 = p2_ref[h].reshape(groups, BF16_ROWS, tm)
            sel = jnp.where(rank < packed_row(n_ref, h, r), p2, jnp.zeros_like(p2))
            w = w + sel * packed_row(p1_ref, h, r)
        rows = slice(r * PEER_NKEYS, (r + 1) * PEER_NKEYS)
        xb = pre_ref[rows, :]
        tb = _gelu_tanh_factor(xb)
        chunks.append((xb + xb * tb) * w.reshape(PEER_NKEYS, tm))
    wact = jnp.concatenate(chunks, axis=0)

    pv = jnp.dot(vt_ref[...], wact, preferred_element_type=F32)
    first = (n % per_tile) == 0
    acc_ref[...] = jnp.where(first, pv, acc_ref[...] + pv)
    pre_ref[...] = pre_next.astype(BF16)

    @pl.when((n >= 0) & (n % per_tile == per_tile - 1))
    def _():
        xo = x1_ref[...] + acc_ref[...].T
        o_ref[...] = _rms(xo, gf_ref[...])


def _peer(h2t, u, vt, rank, p2, n_word, p1_word, x1, gf, *, tm, te):
    d, t = h2t.shape
    n_exp = u.shape[0]
    per_tile = n_exp // te
    rows_per_item = te // PEER_NKEYS
    n_items = (t // tm) * per_tile
    body = functools.partial(_peer_body, per_tile=per_tile, rows_per_item=rows_per_item)

    def item(s, off):
        return jnp.clip(s - 1 + off, 0, n_items - 1)

    tile = lambda s, off: item(s, off) // per_tile
    blk = lambda s, off: item(s, off) % per_tile
    full_spec = pl.BlockSpec((PEER_HEADS, PEER_NKEYS, tm), lambda s: (0, 0, tile(s, 0)))
    rows_spec = pl.BlockSpec((PEER_HEADS, rows_per_item, tm),
                             lambda s: (0, blk(s, 0), tile(s, 0)))
    tok_spec = pl.BlockSpec((tm, d), lambda s: (tile(s, 0), 0))
    return pl.pallas_call(
        body,
        grid=(n_items + 1,),
        in_specs=[
            pl.BlockSpec((d, tm), lambda s: (0, tile(s, 1))),
            pl.BlockSpec((te, d), lambda s: (blk(s, 1), 0)),
            pl.BlockSpec((d, te), lambda s: (0, blk(s, 0))),
            full_spec, full_spec, rows_spec, rows_spec, tok_spec,
            pl.BlockSpec((1, d), lambda s: (0, 0)),
        ],
        out_specs=tok_spec,
        out_shape=jax.ShapeDtypeStruct((t, d), F32),
        scratch_shapes=[pltpu.VMEM((te, tm), BF16), pltpu.VMEM((d, tm), F32)],
        compiler_params=pltpu.CompilerParams(
            dimension_semantics=("arbitrary",),
            vmem_limit_bytes=VMEM_LIMIT),
        name="peer",
    )(h2t, u, vt, rank, p2, n_word, p1_word, x1, gf)


def _block_diag(w):
    heads, dh, _ = w.shape
    eye = jnp.eye(heads, dtype=w.dtype)
    return jnp.einsum("hij,hg->higj", w, eye).reshape(heads * dh, heads * dh)


def _layer(x, norm1_g, w_in, conv_w, conv_b, lru_wa, lru_ba, lru_wx, lru_bx, lru_lambda,
           fox_bf, gn_lru_g, gn_fox_g, w_out, norm2_g, peer_wq, peer_subkeys, peer_u,
           peer_v, out_g):
    bsz, seq, d = x.shape
    width = lru_lambda.shape[0]
    row = lambda v: v.reshape(1, -1).astype(F32)

    w_main = w_in[:, :5 * width].astype(BF16)
    w_fl = jnp.pad(w_in[:, 5 * width:], ((0, 0), (0, LANES - FOX_HEADS))).astype(BF16)
    b_fl = jnp.pad(fox_bf.astype(F32), (0, LANES - FOX_HEADS)).reshape(1, LANES)
    t_blk = FOX_BLOCK
    xr, gr, qt, k, vt, f_t = _inproj(x, row(norm1_g), w_main, w_fl, b_fl, tm=t_blk)

    w_gate = jnp.concatenate([_block_diag(lru_wa), _block_diag(lru_wx)], axis=1).astype(BF16)
    b_gate = jnp.concatenate([lru_ba, lru_bx]).reshape(1, -1).astype(F32)
    y_lru = _rglru(xr, gr, conv_w.astype(F32), row(conv_b), w_gate, b_gate,
                   row(lru_lambda), row(gn_lru_g), ts=256)

    pairs = FOX_HEADS // 2
    f_pair = f_t.reshape(bsz, pairs, 2, seq)
    f_col = f_pair.transpose(0, 1, 3, 2)
    f_row = f_pair.reshape(bsz, pairs, 2, seq // t_blk, t_blk).transpose(0, 1, 3, 2, 4)
    y_fox_t = _fox(qt, k, vt, f_row, f_col)

    t = bsz * seq
    n_hp = 2 * PEER_HEADS
    subkeys = peer_subkeys.reshape(n_hp, PEER_NKEYS, -1).astype(BF16)
    x1, h2t, s1, s2 = _mid(x.reshape(t, d), y_lru.reshape(t, width), y_fox_t,
                           row(gn_fox_g), w_out.astype(BF16), row(norm2_g),
                           peer_wq.T.astype(BF16), subkeys, tm=512)

    n_sel, p1, rank2, p2 = _gates(s1, s2, tm=256)
    out = _peer(h2t, peer_u.astype(BF16), peer_v.T.astype(BF16), rank2, p2, n_sel, p1, x1,
                row(out_g), tm=512, te=2048)
    return out.reshape(bsz, seq, d)


def kernel(x, norm1_g, w_in, conv_w, conv_b, lru_wa, lru_ba, lru_wx, lru_bx, lru_lambda,
           fox_bf, gn_lru_g, gn_fox_g, w_out, norm2_g, peer_wq, peer_subkeys, peer_u,
           peer_v, final_g):
    depth = w_in.shape[0]
    assert depth == 1, "the fused final RMSNorm assumes a single trunk layer"
    return _layer(x, norm1_g[0], w_in[0], conv_w[0], conv_b[0], lru_wa[0], lru_ba[0],
                  lru_wx[0], lru_bx[0], lru_lambda[0], fox_bf[0], gn_lru_g[0], gn_fox_g[0],
                  w_out[0], norm2_g[0], peer_wq[0], peer_subkeys[0], peer_u[0], peer_v[0],
                  final_g)
```

```python
import functools

import jax
import jax.numpy as jnp
import numpy as np
from jax import lax
from jax.experimental import pallas as pl
from jax.experimental.pallas import tpu as pltpu

F32 = jnp.float32
BF16 = jnp.bfloat16

RMS_EPS = 1e-6
LRU_C = 8.0
CONV_WIDTH = 4
LRU_HEADS = 8
FOX_HEADS = 8
PEER_HEADS = 8
PEER_NKEYS = 128
PEER_TOPK = 16
FOX_BLOCK = 512

LANES = 128
SUBLANES = 8
VMEM_LIMIT = 56 * 1024 * 1024

NEG_INF = float("-inf")
LOG2E = 1.4426950408889634
_GELU_K = 0.7978845608028654
BF16_ROWS = 16


def _rms(x, g):
    return x * lax.rsqrt(jnp.mean(x * x, axis=-1, keepdims=True) + RMS_EPS) * g


def _gelu_tanh_factor(x):
    return jnp.tanh(x * (_GELU_K + (_GELU_K * 0.044715) * (x * x)))


def _inproj_body(x_ref, g_ref, w_ref, wfl_ref, bf_ref,
                 xr_ref, gr_ref, qt_ref, k_ref, vt_ref, ft_ref, carry_ref,
                 *, tm, width, q_scale):
    s = pl.program_id(1)

    @pl.when(s == 0)
    def _():
        carry_ref[...] = jnp.zeros_like(carry_ref)

    hb = _rms(x_ref[0], g_ref[...]).astype(BF16)

    def piece(i):
        return jnp.dot(hb, w_ref[:, i * width:(i + 1) * width],
                       preferred_element_type=F32)

    xr_ref[0] = piece(0)
    gr_ref[0] = piece(1)
    qt_ref[0, 0] = (piece(2) * (q_scale * LOG2E)).T.astype(BF16)
    k_ref[0] = piece(3).astype(BF16)
    vt_ref[0, 0] = piece(4).T.astype(BF16)

    fl = jnp.dot(hb, wfl_ref[...], preferred_element_type=F32)
    c = jax.nn.log_sigmoid(fl + bf_ref[...]) * LOG2E
    row = lax.broadcasted_iota(jnp.int32, c.shape, 0)
    sh = 1
    while sh < tm:
        c = c + jnp.where(row >= sh, pltpu.roll(c, sh, axis=0), 0.0)
        sh *= 2
    f = c + carry_ref[...]
    carry_ref[...] = f[tm - 1:tm, :]
    ft_ref[0] = f.T[0:FOX_HEADS, :]


def _inproj(x, g, w_main, w_fl, b_fl, *, tm):
    bsz, seq, d = x.shape
    width = w_main.shape[1] // 5
    body = functools.partial(_inproj_body, tm=tm, width=width,
                             q_scale=(width // FOX_HEADS) ** -0.5)
    tok = lambda b, s: (b, s, 0)
    const = lambda b, s: (0, 0)
    big = jax.ShapeDtypeStruct((bsz, seq, width), F32)
    bigb = jax.ShapeDtypeStruct((bsz, seq, width), BF16)
    blocked_t = jax.ShapeDtypeStruct((bsz, seq // tm, width, tm), BF16)
    return pl.pallas_call(
        body,
        grid=(bsz, seq // tm),
        in_specs=[
            pl.BlockSpec((1, tm, d), tok),
            pl.BlockSpec((1, d), const),
            pl.BlockSpec(w_main.shape, const),
            pl.BlockSpec(w_fl.shape, const),
            pl.BlockSpec((1, LANES), const),
        ],
        out_specs=[pl.BlockSpec((1, tm, width), tok),
                   pl.BlockSpec((1, tm, width), tok),
                   pl.BlockSpec((1, 1, width, tm), lambda b, s: (b, s, 0, 0)),
                   pl.BlockSpec((1, tm, width), tok),
                   pl.BlockSpec((1, 1, width, tm), lambda b, s: (b, s, 0, 0)),
                   pl.BlockSpec((1, FOX_HEADS, tm), lambda b, s: (b, 0, s))],
        out_shape=[big, big, blocked_t, bigb, blocked_t,
                   jax.ShapeDtypeStruct((bsz, FOX_HEADS, seq), F32)],
        scratch_shapes=[pltpu.VMEM((1, LANES), F32)],
        compiler_params=pltpu.CompilerParams(
            dimension_semantics=("parallel", "arbitrary"),
            vmem_limit_bytes=VMEM_LIMIT),
        name="inproj",
    )(x, g, w_main, w_fl, b_fl)


def _rglru_body(xr_ref, gr_ref, cw_ref, cb_ref, wg_ref, bg_ref, lam_ref, gn_ref,
                y_ref, xbuf_ref, hc_ref, *, ts, width):
    s = pl.program_id(1)

    @pl.when(s == 0)
    def _():
        xbuf_ref[0:SUBLANES, :] = jnp.zeros((SUBLANES, width), F32)
        hc_ref[...] = jnp.zeros_like(hc_ref)

    x = xr_ref[0]
    xbuf_ref[SUBLANES:SUBLANES + ts, :] = x
    cw = cw_ref[...]
    conv = cb_ref[...] + cw[CONV_WIDTH - 1:CONV_WIDTH, :] * x
    for back in range(1, CONV_WIDTH):
        k = CONV_WIDTH - 1 - back
        conv = conv + cw[k:k + 1, :] * xbuf_ref[SUBLANES - back:SUBLANES - back + ts, :]
    xbuf_ref[0:SUBLANES, :] = x[ts - SUBLANES:ts, :]

    gates = jnp.dot(conv.astype(BF16), wg_ref[...],
                    preferred_element_type=F32) + bg_ref[...]
    r = jax.nn.sigmoid(gates[:, :width])
    i = jax.nn.sigmoid(gates[:, width:])
    log_a = (-LRU_C) * r * jax.nn.softplus(-lam_ref[...])
    a = jnp.exp(log_a)
    u = jnp.sqrt(1.0 - a * a) * (i * conv)

    row = lax.broadcasted_iota(jnp.int32, a.shape, 0)
    sh = 1
    while sh < ts:
        keep = row >= sh
        a_s = jnp.where(keep, pltpu.roll(a, sh, axis=0), 1.0)
        u_s = jnp.where(keep, pltpu.roll(u, sh, axis=0), 0.0)
        u = a * u_s + u
        a = a * a_s
        sh *= 2
    h = u + a * hc_ref[...]
    hc_ref[...] = h[ts - 1:ts, :]

    y = h * jax.nn.gelu(gr_ref[0])
    y_ref[0] = _rms(y, gn_ref[...]).astype(BF16)


def _rglru(xr, gr, conv_w, conv_b, w_gate, b_gate, lam, gn, *, ts):
    bsz, seq, width = xr.shape
    body = functools.partial(_rglru_body, ts=ts, width=width)
    tok = lambda b, s: (b, s, 0)
    const = lambda b, s: (0, 0)
    return pl.pallas_call(
        body,
        grid=(bsz, seq // ts),
        in_specs=[
            pl.BlockSpec((1, ts, width), tok),
            pl.BlockSpec((1, ts, width), tok),
            pl.BlockSpec(conv_w.shape, const),
            pl.BlockSpec((1, width), const),
            pl.BlockSpec(w_gate.shape, const),
            pl.BlockSpec((1, 2 * width), const),
            pl.BlockSpec((1, width), const),
            pl.BlockSpec((1, width), const),
        ],
        out_specs=pl.BlockSpec((1, ts, width), tok),
        out_shape=jax.ShapeDtypeStruct((bsz, seq, width), BF16),
        scratch_shapes=[pltpu.VMEM((ts + SUBLANES, width), F32),
                        pltpu.VMEM((1, width), F32)],
        compiler_params=pltpu.CompilerParams(
            dimension_semantics=("parallel", "arbitrary"),
            vmem_limit_bytes=VMEM_LIMIT),
        name="rglru",
    )(xr, gr, conv_w, conv_b, w_gate, b_gate, lam, gn)


_ST_M, _ST_L = 0, 2
_PB_ALPHA, _PB_LFIN = 0, 4


def _fox_body(qi_off_ref, kj_off_ref, qi_diag_ref, qt_ref, k_ref, vt_ref, fq_ref, fk_ref,
              o_ref, s0_ref, s1_ref, p0_ref, p1_ref, p2_ref, p3_ref, st_ref, pbo_ref, pbd_ref,
              acc_ref, mb_ref,
              *, n_off, n_diag, tq, tk, dh):
    row_t = lax.broadcasted_iota(jnp.int32, (tk, tq), 0)
    col_t = lax.broadcasted_iota(jnp.int32, (tk, tq), 1)
    mb_ref[...] = jnp.where(row_t <= col_t, 0.0, NEG_INF)
    st_ref[:, _ST_M:_ST_M + 2, :] = jnp.full((st_ref.shape[0], 2, tq), NEG_INF, F32)
    st_ref[:, _ST_L:_ST_L + 2, :] = jnp.zeros((st_ref.shape[0], 2, tq), F32)
    acc_ref[...] = jnp.zeros_like(acc_ref)
    head_row = lax.broadcasted_iota(jnp.int32, (2 * dh, 1), 0)

    def stage_a(qi, kj, s_ref):
        qt = qt_ref[0, qi]
        kb = k_ref[0, pl.ds(pl.multiple_of(kj * tk, tk), tk), :]
        for h in range(2):
            in_head = (head_row >= h * dh) & (head_row < (h + 1) * dh)
            qm = jnp.where(in_head, qt, jnp.zeros_like(qt))
            s_ref[h] = jnp.dot(kb, qm, preferred_element_type=F32)

    def stage_b(qi, kj, s_ref, p_ref, pb_ref, buf, diagonal):
        fq2 = fq_ref[0, 0, qi]
        for h in range(2):
            fk = fk_ref[0, 0, pl.ds(pl.multiple_of(kj * tk, tk), tk), h:h + 1]
            fq = fq2[h:h + 1, :]
            z = s_ref[h] - fk
            if diagonal:
                z = z + mb_ref[...]
            mz = jnp.max(z, axis=0, keepdims=True) + fq
            m_prev = st_ref[qi, _ST_M + h:_ST_M + h + 1, :]
            m_new = jnp.maximum(m_prev, mz)
            alpha = jnp.exp2(m_prev - m_new)
            p = jnp.exp2(z - (m_new - fq))
            l_new = (alpha * st_ref[qi, _ST_L + h:_ST_L + h + 1, :]
                     + jnp.sum(p, axis=0, keepdims=True))
            p_ref[h] = p.astype(BF16)
            st_ref[qi, _ST_M + h:_ST_M + h + 1, :] = m_new
            st_ref[qi, _ST_L + h:_ST_L + h + 1, :] = l_new
            ra = _PB_ALPHA + 2 * buf + h
            pb_ref[ra:ra + 1, :] = alpha
            rl = _PB_LFIN + 2 * buf + h
            pb_ref[rl:rl + 1, :] = l_new

    def stage_c(qi, kj, p_ref, pb_ref, buf, diagonal):
        vt = vt_ref[0, kj]
        for h in range(2):
            rows = slice(h * dh, (h + 1) * dh)
            pv = jnp.dot(vt[rows, :], p_ref[h], preferred_element_type=F32)
            ra = _PB_ALPHA + 2 * buf + h
            acc = pb_ref[ra:ra + 1, :] * acc_ref[qi, rows, :] + pv
            acc_ref[qi, rows, :] = acc
            if diagonal:
                rl = _PB_LFIN + 2 * buf + h
                o_ref[0, qi, rows, :] = acc / pb_ref[rl:rl + 1, :]

    def run(n_items, item, diagonal, pa_ref, pz_ref, pb_ref):
        stage_a(*item(0), s0_ref)

        def pair(jj, carry):
            j0 = 2 * jj
            stage_a(*item(j0 + 1), s1_ref)
            stage_b(*item(j0), s0_ref, pa_ref, pb_ref, 0, diagonal)
            stage_c(*item(jnp.maximum(j0 - 1, 0)), pz_ref, pb_ref, 1, diagonal)
            stage_a(*item(jnp.minimum(j0 + 2, n_items - 1)), s0_ref)
            stage_b(*item(j0 + 1), s1_ref, pz_ref, pb_ref, 1, diagonal)
            stage_c(*item(j0), pa_ref, pb_ref, 0, diagonal)
            return carry

        lax.fori_loop(0, n_items // 2, pair, 0)
        stage_c(*item(n_items - 1), pz_ref, pb_ref, 1, diagonal)

    p1_ref[...] = jnp.zeros_like(p1_ref)
    p3_ref[...] = jnp.zeros_like(p3_ref)
    pbo_ref[...] = jnp.ones_like(pbo_ref)
    pbd_ref[...] = jnp.ones_like(pbd_ref)
    run(n_off, lambda j: (qi_off_ref[j], kj_off_ref[j]), False, p0_ref, p1_ref, pbo_ref)
    run(n_diag, lambda j: (qi_diag_ref[j], qi_diag_ref[j]), True, p2_ref, p3_ref, pbd_ref)


def _fox(qt, k, vt, f_row, f_col):
    bsz, nq, width, tq = qt.shape
    seq = k.shape[1]
    dh = width // FOX_HEADS
    pairs = FOX_HEADS // 2
    qi_off = np.concatenate([np.full(i, i, np.int32) for i in range(1, nq)])
    kj_off = np.concatenate([np.arange(i, dtype=np.int32) for i in range(1, nq)])
    qi_diag = np.arange(nq, dtype=np.int32)
    assert len(qi_off) % 2 == 0 and nq % 2 == 0, "the pipelined loops take two items per trip"
    body = functools.partial(_fox_body, n_off=len(qi_off), n_diag=nq, tq=tq, tk=tq, dh=dh)
    blk_t = pl.BlockSpec((1, nq, 2 * dh, tq), lambda b, p, *_: (b, 0, p, 0))
    grid_spec = pltpu.PrefetchScalarGridSpec(
        num_scalar_prefetch=3,
        grid=(bsz, pairs),
        in_specs=[
            blk_t,
            pl.BlockSpec((1, seq, 2 * dh), lambda b, p, *_: (b, 0, p)),
            blk_t,
            pl.BlockSpec((1, 1, nq, 2, tq), lambda b, p, *_: (b, p, 0, 0, 0)),
            pl.BlockSpec((1, 1, seq, 2), lambda b, p, *_: (b, p, 0, 0)),
        ],
        out_specs=blk_t,
        scratch_shapes=[pltpu.VMEM((2, tq, tq), F32),
                        pltpu.VMEM((2, tq, tq), F32),
                        pltpu.VMEM((2, tq, tq), BF16),
                        pltpu.VMEM((2, tq, tq), BF16),
                        pltpu.VMEM((2, tq, tq), BF16),
                        pltpu.VMEM((2, tq, tq), BF16),
                        pltpu.VMEM((nq, SUBLANES, tq), F32),
                        pltpu.VMEM((SUBLANES, tq), F32),
                        pltpu.VMEM((SUBLANES, tq), F32),
                        pltpu.VMEM((nq, 2 * dh, tq), F32),
                        pltpu.VMEM((tq, tq), F32)],
    )
    return pl.pallas_call(
        body,
        grid_spec=grid_spec,
        out_shape=jax.ShapeDtypeStruct((bsz, nq, width, tq), F32),
        compiler_params=pltpu.CompilerParams(
            dimension_semantics=("parallel", "parallel"),
            vmem_limit_bytes=VMEM_LIMIT),
        name="fox",
    )(jnp.asarray(qi_off), jnp.asarray(kj_off), jnp.asarray(qi_diag), qt, k, vt, f_row, f_col)


def _mid_body(x_ref, yl_ref, yf_ref, gnf_ref, wo_ref, g2_ref, wqt_ref, sk_ref,
              x1_ref, h2t_ref, s1_ref, s2_ref, *, half):
    yfn = _rms(yf_ref[0, 0].T, gnf_ref[...]).astype(BF16)
    mix = jnp.dot(yl_ref[...], wo_ref[0:half, :], preferred_element_type=F32)
    mix = mix + jnp.dot(yfn, wo_ref[half:2 * half, :], preferred_element_type=F32)
    x1 = x_ref[...] + mix
    x1_ref[...] = x1
    h2t = _rms(x1, g2_ref[...]).T.astype(BF16)
    h2t_ref[...] = h2t
    qpt = jnp.dot(wqt_ref[...], h2t, preferred_element_type=F32)
    for hp in range(2 * PEER_HEADS):
        qs = qpt[hp * LANES:(hp + 1) * LANES, :].astype(BF16)
        sc = jnp.dot(sk_ref[hp], qs, preferred_element_type=F32)
        if hp % 2 == 0:
            s1_ref[hp // 2] = sc
        else:
            s2_ref[hp // 2] = sc


def _mid(x2d, y_lru, y_fox_t, gn_fox, w_out, g2, wq_t, subkeys, *, tm):
    t, d = x2d.shape
    half = y_lru.shape[1]
    _, n_blk, _, t_blk = y_fox_t.shape
    per_blk = t_blk // tm
    body = functools.partial(_mid_body, half=half)
    tok = lambda i: (i, 0)
    const2 = lambda i: (0, 0)
    sc_shape = jax.ShapeDtypeStruct((PEER_HEADS, PEER_NKEYS, t), F32)
    sc_spec = pl.BlockSpec((PEER_HEADS, PEER_NKEYS, tm), lambda i: (0, 0, i))
    return pl.pallas_call(
        body,
        grid=(t // tm,),
        in_specs=[
            pl.BlockSpec((tm, d), tok),
            pl.BlockSpec((tm, half), tok),
            pl.BlockSpec((1, 1, half, tm),
                         lambda i: (i // (per_blk * n_blk), (i // per_blk) % n_blk, 0,
                                    i % per_blk)),
            pl.BlockSpec((1, half), const2),
            pl.BlockSpec(w_out.shape, const2),
            pl.BlockSpec((1, d), const2),
            pl.BlockSpec(wq_t.shape, const2),
            pl.BlockSpec(subkeys.shape, lambda i: (0, 0, 0)),
        ],
        out_specs=[pl.BlockSpec((tm, d), tok),
                   pl.BlockSpec((d, tm), lambda i: (0, i)),
                   sc_spec, sc_spec],
        out_shape=[jax.ShapeDtypeStruct((t, d), F32),
                   jax.ShapeDtypeStruct((d, t), BF16),
                   sc_shape, sc_shape],
        compiler_params=pltpu.CompilerParams(
            dimension_semantics=("parallel",),
            vmem_limit_bytes=VMEM_LIMIT),
        name="mid",
    )(x2d, y_lru, y_fox_t, gn_fox, w_out, g2, wq_t, subkeys)


_CAND = [(a, b) for a in range(PEER_TOPK) for b in range(PEER_TOPK)
         if (a + 1) * (b + 1) <= PEER_TOPK]
_CAND_ROWS = -(-len(_CAND) // SUBLANES) * SUBLANES


def _top_values(x, n, want_rank=False):
    vals = []
    rank = jnp.full(x.shape, float(n), F32) if want_rank else None
    for r in range(n):
        mx = jnp.max(x, axis=0, keepdims=True)
        vals.append(mx)
        hit = x == mx
        x = jnp.where(hit, NEG_INF, x)
        if want_rank:
            rank = jnp.where(hit, float(r), rank)
    return vals, rank


def _gates_body(s1_ref, s2_ref, n_ref, p1_ref, rank_ref, p2_ref, cand_ref):
    s1 = s1_ref[0]
    s2 = s2_ref[0]
    a, _ = _top_values(s1, PEER_TOPK)
    b, rank = _top_values(s2, PEER_TOPK, want_rank=True)
    cand_ref[...] = jnp.full_like(cand_ref, NEG_INF)
    sums = [a[ia] + b[ib] for ia, ib in _CAND]
    for r, v in enumerate(sums):
        cand_ref[r:r + 1, :] = v
    c, _ = _top_values(cand_ref[...], PEER_TOPK)
    z = jnp.zeros_like(c[0])
    for r in range(PEER_TOPK):
        z = z + jnp.exp(c[r] - c[0])
    tau = c[PEER_TOPK - 1]
    picked = [jnp.zeros_like(tau) for _ in range(PEER_TOPK)]
    for (ia, _), v in zip(_CAND, sums):
        picked[ia] = picked[ia] + jnp.where(v >= tau, 1.0, 0.0)
    n = jnp.zeros_like(s1)
    for ia in range(PEER_TOPK):
        n = jnp.where(s1 == a[ia], picked[ia], n)
    n_ref[0] = n
    p1_ref[0] = 0.5 * jnp.exp(s1 - a[0]) / z
    rank_ref[0] = rank.astype(BF16)
    p2_ref[0] = jnp.exp(s2 - b[0]).astype(BF16)


def _gates(s1, s2, *, tm):
    heads, nk, t = s1.shape
    spec = pl.BlockSpec((1, nk, tm), lambda i, h: (h, 0, i))
    shape = jax.ShapeDtypeStruct(s1.shape, F32)
    shape_b = jax.ShapeDtypeStruct(s1.shape, BF16)
    return pl.pallas_call(
        _gates_body,
        grid=(t // tm, heads),
        in_specs=[spec, spec],
        out_specs=[spec, spec, spec, spec],
        out_shape=[shape, shape, shape_b, shape_b],
        scratch_shapes=[pltpu.VMEM((_CAND_ROWS, tm), F32)],
        compiler_params=pltpu.CompilerParams(
            dimension_semantics=("parallel", "parallel"),
            vmem_limit_bytes=VMEM_LIMIT),
        name="gates",
    )(s1, s2)


def _peer_body(h2_ref, u_ref, vt_ref, rank_ref, p2_ref, n_ref, p1_ref, x1_ref, gf_ref,
               o_ref, pre_ref, acc_ref, *, per_tile, rows_per_item):
    s = pl.program_id(0)
    n = s - 1

    @pl.when(s == 0)
    def _():
        pre_ref[...] = jnp.zeros_like(pre_ref)
        acc_ref[...] = jnp.zeros_like(acc_ref)

    pre_next = jnp.dot(u_ref[...], h2_ref[...], preferred_element_type=F32)

    tm = pre_ref.shape[1]

    def packed_row(ref, h, r):
        tile = jnp.broadcast_to(ref[h, r:r + 1, :], (BF16_ROWS, tm)).astype(BF16)
        return tile[None]

    chunks = []
    groups = PEER_NKEYS // BF16_ROWS
    for r in range(rows_per_item):
        w = jnp.zeros((groups, BF16_ROWS, tm), BF16)
        for h in range(PEER_HEADS):
            rank = rank_ref[h].reshape(groups, BF16_ROWS, tm)
            p2 = p2_ref[h].reshape(groups, BF16_ROWS, tm)
            sel = jnp.where(rank < packed_row(n_ref, h, r), p2, jnp.zeros_like(p2))
            w = w + sel * packed_row(p1_ref, h, r)
        rows = slice(r * PEER_NKEYS, (r + 1) * PEER_NKEYS)
        xb = pre_ref[rows, :].astype(BF16)
        tb = _gelu_tanh_factor(xb)
        chunks.append((xb + xb * tb) * w.reshape(PEER_NKEYS, tm))
    wact = jnp.concatenate(chunks, axis=0)

    pv = jnp.dot(vt_ref[...], wact, preferred_element_type=F32)
    first = (n % per_tile) == 0
    acc_ref[...] = jnp.where(first, pv, acc_ref[...] + pv)
    pre_ref[...] = pre_next

    @pl.when((n >= 0) & (n % per_tile == per_tile - 1))
    def _():
        xo = x1_ref[...] + acc_ref[...].T
        o_ref[...] = _rms(xo, gf_ref[...])


def _peer(h2t, u, vt, rank, p2, n_sel, p1, x1, gf, *, tm, te):
    d, t = h2t.shape
    n_exp = u.shape[0]
    per_tile = n_exp // te
    rows_per_item = te // PEER_NKEYS
    n_items = (t // tm) * per_tile
    body = functools.partial(_peer_body, per_tile=per_tile, rows_per_item=rows_per_item)

    def item(s, off):
        return jnp.clip(s - 1 + off, 0, n_items - 1)

    tile = lambda s, off: item(s, off) // per_tile
    blk = lambda s, off: item(s, off) % per_tile
    full_spec = pl.BlockSpec((PEER_HEADS, PEER_NKEYS, tm), lambda s: (0, 0, tile(s, 0)))
    rows_spec = pl.BlockSpec((PEER_HEADS, rows_per_item, tm),
                             lambda s: (0, blk(s, 0), tile(s, 0)))
    tok_spec = pl.BlockSpec((tm, d), lambda s: (tile(s, 0), 0))
    return pl.pallas_call(
        body,
        grid=(n_items + 1,),
        in_specs=[
            pl.BlockSpec((d, tm), lambda s: (0, tile(s, 1))),
            pl.BlockSpec((te, d), lambda s: (blk(s, 1), 0)),
            pl.BlockSpec((d, te), lambda s: (0, blk(s, 0))),
            full_spec, full_spec, rows_spec, rows_spec, tok_spec,
            pl.BlockSpec((1, d), lambda s: (0, 0)),
        ],
        out_specs=tok_spec,
        out_shape=jax.ShapeDtypeStruct((t, d), F32),
        scratch_shapes=[pltpu.VMEM((te, tm), F32), pltpu.VMEM((d, tm), F32)],
        compiler_params=pltpu.CompilerParams(
            dimension_semantics=("arbitrary",),
            vmem_limit_bytes=VMEM_LIMIT),
        name="peer",
    )(h2t, u, vt, rank, p2, n_sel, p1, x1, gf)


def _block_diag(w):
    heads, dh, _ = w.shape
    eye = jnp.eye(heads, dtype=w.dtype)
    return jnp.einsum("hij,hg->higj", w, eye).reshape(heads * dh, heads * dh)


def _layer(x, norm1_g, w_in, conv_w, conv_b, lru_wa, lru_ba, lru_wx, lru_bx, lru_lambda,
           fox_bf, gn_lru_g, gn_fox_g, w_out, norm2_g, peer_wq, peer_subkeys, peer_u,
           peer_v, out_g):
    bsz, seq, d = x.shape
    width = lru_lambda.shape[0]
    row = lambda v: v.reshape(1, -1).astype(F32)

    w_main = w_in[:, :5 * width].astype(BF16)
    w_fl = jnp.pad(w_in[:, 5 * width:], ((0, 0), (0, LANES - FOX_HEADS))).astype(BF16)
    b_fl = jnp.pad(fox_bf.astype(F32), (0, LANES - FOX_HEADS)).reshape(1, LANES)
    t_blk = FOX_BLOCK
    xr, gr, qt, k, vt, f_t = _inproj(x, row(norm1_g), w_main, w_fl, b_fl, tm=t_blk)

    w_gate = jnp.concatenate([_block_diag(lru_wa), _block_diag(lru_wx)], axis=1).astype(BF16)
    b_gate = jnp.concatenate([lru_ba, lru_bx]).reshape(1, -1).astype(F32)
    y_lru = _rglru(xr, gr, conv_w.astype(F32), row(conv_b), w_gate, b_gate,
                   row(lru_lambda), row(gn_lru_g), ts=256)

    pairs = FOX_HEADS // 2
    f_pair = f_t.reshape(bsz, pairs, 2, seq)
    f_col = f_pair.transpose(0, 1, 3, 2)
    f_row = f_pair.reshape(bsz, pairs, 2, seq // t_blk, t_blk).transpose(0, 1, 3, 2, 4)
    y_fox_t = _fox(qt, k, vt, f_row, f_col)

    t = bsz * seq
    n_hp = 2 * PEER_HEADS
    subkeys = peer_subkeys.reshape(n_hp, PEER_NKEYS, -1).astype(BF16)
    x1, h2t, s1, s2 = _mid(x.reshape(t, d), y_lru.reshape(t, width), y_fox_t,
                           row(gn_fox_g), w_out.astype(BF16), row(norm2_g),
                           peer_wq.T.astype(BF16), subkeys, tm=512)

    n_sel, p1, rank2, p2 = _gates(s1, s2, tm=256)
    out = _peer(h2t, peer_u.astype(BF16), peer_v.T.astype(BF16), rank2, p2, n_sel, p1, x1,
                row(out_g), tm=512, te=2048)
    return out.reshape(bsz, seq, d)


def kernel(x, norm1_g, w_in, conv_w, conv_b, lru_wa, lru_ba, lru_wx, lru_bx, lru_lambda,
           fox_bf, gn_lru_g, gn_fox_g, w_out, norm2_g, peer_wq, peer_subkeys, peer_u,
           peer_v, final_g):
    depth = w_in.shape[0]
    assert depth == 1, "the fused final RMSNorm assumes a single trunk layer"
    return _layer(x, norm1_g[0], w_in[0], conv_w[0], conv_b[0], lru_wa[0], lru_ba[0],
                  lru_wx[0], lru_bx[0], lru_lambda[0], fox_bf[0], gn_lru_g[0], gn_fox_g[0],
                  w_out[0], norm2_g[0], peer_wq[0], peer_subkeys[0], peer_u[0], peer_v[0],
                  final_g)
```

```python
import functools

import jax
import jax.numpy as jnp
import numpy as np
from jax import lax
from jax.experimental import pallas as pl
from jax.experimental.pallas import tpu as pltpu

F32 = jnp.float32
BF16 = jnp.bfloat16

RMS_EPS = 1e-6
LRU_C = 8.0
CONV_WIDTH = 4
LRU_HEADS = 8
FOX_HEADS = 8
PEER_HEADS = 8
PEER_NKEYS = 128
PEER_TOPK = 16
FOX_BLOCK = 512

LANES = 128
SUBLANES = 8
VMEM_LIMIT = 56 * 1024 * 1024

NEG_INF = float("-inf")
LOG2E = 1.4426950408889634
_GELU_K = 0.7978845608028654
BF16_ROWS = 16


def _rms(x, g):
    return x * lax.rsqrt(jnp.mean(x * x, axis=-1, keepdims=True) + RMS_EPS) * g


def _gelu_tanh_factor(x):
    return jnp.tanh(x * (_GELU_K + (_GELU_K * 0.044715) * (x * x)))


def _inproj_body(x_ref, g_ref, w_ref, wfl_ref, bf_ref,
                 xr_ref, gr_ref, qt_ref, k_ref, vt_ref, ft_ref, carry_ref,
                 *, tm, width, q_scale):
    s = pl.program_id(1)

    @pl.when(s == 0)
    def _():
        carry_ref[...] = jnp.zeros_like(carry_ref)

    hb = _rms(x_ref[0], g_ref[...]).astype(BF16)

    def piece(i):
        return jnp.dot(hb, w_ref[:, i * width:(i + 1) * width],
                       preferred_element_type=F32)

    xr_ref[0] = piece(0)
    gr_ref[0] = piece(1)
    qt_ref[0, 0] = (piece(2) * (q_scale * LOG2E)).T.astype(BF16)
    k_ref[0] = piece(3).astype(BF16)
    vt_ref[0, 0] = piece(4).T.astype(BF16)

    fl = jnp.dot(hb, wfl_ref[...], preferred_element_type=F32)
    c = jax.nn.log_sigmoid(fl + bf_ref[...]) * LOG2E
    row = lax.broadcasted_iota(jnp.int32, c.shape, 0)
    sh = 1
    while sh < tm:
        c = c + jnp.where(row >= sh, pltpu.roll(c, sh, axis=0), 0.0)
        sh *= 2
    f = c + carry_ref[...]
    carry_ref[...] = f[tm - 1:tm, :]
    ft_ref[0] = f.T[0:FOX_HEADS, :]


def _inproj(x, g, w_main, w_fl, b_fl, *, tm):
    bsz, seq, d = x.shape
    width = w_main.shape[1] // 5
    body = functools.partial(_inproj_body, tm=tm, width=width,
                             q_scale=(width // FOX_HEADS) ** -0.5)
    tok = lambda b, s: (b, s, 0)
    const = lambda b, s: (0, 0)
    big = jax.ShapeDtypeStruct((bsz, seq, width), F32)
    bigb = jax.ShapeDtypeStruct((bsz, seq, width), BF16)
    blocked_t = jax.ShapeDtypeStruct((bsz, seq // tm, width, tm), BF16)
    return pl.pallas_call(
        body,
        grid=(bsz, seq // tm),
        in_specs=[
            pl.BlockSpec((1, tm, d), tok),
            pl.BlockSpec((1, d), const),
            pl.BlockSpec(w_main.shape, const),
            pl.BlockSpec(w_fl.shape, const),
            pl.BlockSpec((1, LANES), const),
        ],
        out_specs=[pl.BlockSpec((1, tm, width), tok),
                   pl.BlockSpec((1, tm, width), tok),
                   pl.BlockSpec((1, 1, width, tm), lambda b, s: (b, s, 0, 0)),
                   pl.BlockSpec((1, tm, width), tok),
                   pl.BlockSpec((1, 1, width, tm), lambda b, s: (b, s, 0, 0)),
                   pl.BlockSpec((1, FOX_HEADS, tm), lambda b, s: (b, 0, s))],
        out_shape=[big, big, blocked_t, bigb, blocked_t,
                   jax.ShapeDtypeStruct((bsz, FOX_HEADS, seq), F32)],
        scratch_shapes=[pltpu.VMEM((1, LANES), F32)],
        compiler_params=pltpu.CompilerParams(
            dimension_semantics=("parallel", "arbitrary"),
            vmem_limit_bytes=VMEM_LIMIT),
        name="inproj",
    )(x, g, w_main, w_fl, b_fl)


def _rglru_body(xr_ref, gr_ref, cw_ref, cb_ref, wg_ref, bg_ref, lam_ref, gn_ref,
                y_ref, xbuf_ref, hc_ref, *, ts, width):
    s = pl.program_id(1)

    @pl.when(s == 0)
    def _():
        xbuf_ref[0:SUBLANES, :] = jnp.zeros((SUBLANES, width), F32)
        hc_ref[...] = jnp.zeros_like(hc_ref)

    x = xr_ref[0]
    xbuf_ref[SUBLANES:SUBLANES + ts, :] = x
    cw = cw_ref[...]
    conv = cb_ref[...] + cw[CONV_WIDTH - 1:CONV_WIDTH, :] * x
    for back in range(1, CONV_WIDTH):
        k = CONV_WIDTH - 1 - back
        conv = conv + cw[k:k + 1, :] * xbuf_ref[SUBLANES - back:SUBLANES - back + ts, :]
    xbuf_ref[0:SUBLANES, :] = x[ts - SUBLANES:ts, :]

    gates = jnp.dot(conv.astype(BF16), wg_ref[...],
                    preferred_element_type=F32) + bg_ref[...]
    r = jax.nn.sigmoid(gates[:, :width])
    i = jax.nn.sigmoid(gates[:, width:])
    log_a = (-LRU_C) * r * jax.nn.softplus(-lam_ref[...])
    a = jnp.exp(log_a)
    u = jnp.sqrt(1.0 - a * a) * (i * conv)

    row = lax.broadcasted_iota(jnp.int32, a.shape, 0)
    sh = 1
    while sh < ts:
        keep = row >= sh
        a_s = jnp.where(keep, pltpu.roll(a, sh, axis=0), 1.0)
        u_s = jnp.where(keep, pltpu.roll(u, sh, axis=0), 0.0)
        u = a * u_s + u
        a = a * a_s
        sh *= 2
    h = u + a * hc_ref[...]
    hc_ref[...] = h[ts - 1:ts, :]

    y = h * jax.nn.gelu(gr_ref[0])
    y_ref[0] = _rms(y, gn_ref[...]).astype(BF16)


def _rglru(xr, gr, conv_w, conv_b, w_gate, b_gate, lam, gn, *, ts):
    bsz, seq, width = xr.shape
    body = functools.partial(_rglru_body, ts=ts, width=width)
    tok = lambda b, s: (b, s, 0)
    const = lambda b, s: (0, 0)
    return pl.pallas_call(
        body,
        grid=(bsz, seq // ts),
        in_specs=[
            pl.BlockSpec((1, ts, width), tok),
            pl.BlockSpec((1, ts, width), tok),
            pl.BlockSpec(conv_w.shape, const),
            pl.BlockSpec((1, width), const),
            pl.BlockSpec(w_gate.shape, const),
            pl.BlockSpec((1, 2 * width), const),
            pl.BlockSpec((1, width), const),
            pl.BlockSpec((1, width), const),
        ],
        out_specs=pl.BlockSpec((1, ts, width), tok),
        out_shape=jax.ShapeDtypeStruct((bsz, seq, width), BF16),
        scratch_shapes=[pltpu.VMEM((ts + SUBLANES, width), F32),
                        pltpu.VMEM((1, width), F32)],
        compiler_params=pltpu.CompilerParams(
            dimension_semantics=("parallel", "arbitrary"),
            vmem_limit_bytes=VMEM_LIMIT),
        name="rglru",
    )(xr, gr, conv_w, conv_b, w_gate, b_gate, lam, gn)


_ST_M, _ST_L = 0, 2
_PB_ALPHA, _PB_LFIN = 0, 4


def _fox_body(qi_off_ref, kj_off_ref, qi_diag_ref, qt_ref, k_ref, vt_ref, fq_ref, fk_ref,
              o_ref, s0_ref, s1_ref, p0_ref, p1_ref, p2_ref, p3_ref, st_ref, pbo_ref, pbd_ref,
              acc_ref, mb_ref,
              *, n_off, n_diag, tq, tk, dh):
    row_t = lax.broadcasted_iota(jnp.int32, (tk, tq), 0)
    col_t = lax.broadcasted_iota(jnp.int32, (tk, tq), 1)
    mb_ref[...] = jnp.where(row_t <= col_t, 0.0, NEG_INF)
    st_ref[:, _ST_M:_ST_M + 2, :] = jnp.full((st_ref.shape[0], 2, tq), NEG_INF, F32)
    st_ref[:, _ST_L:_ST_L + 2, :] = jnp.zeros((st_ref.shape[0], 2, tq), F32)
    acc_ref[...] = jnp.zeros_like(acc_ref)
    head_row = lax.broadcasted_iota(jnp.int32, (2 * dh, 1), 0)

    def stage_a(qi, kj, s_ref):
        qt = qt_ref[0, qi]
        kb = k_ref[0, pl.ds(pl.multiple_of(kj * tk, tk), tk), :]
        for h in range(2):
            in_head = (head_row >= h * dh) & (head_row < (h + 1) * dh)
            qm = jnp.where(in_head, qt, jnp.zeros_like(qt))
            s_ref[h] = jnp.dot(kb, qm, preferred_element_type=F32)

    def stage_b(qi, kj, s_ref, p_ref, pb_ref, buf, diagonal):
        fq2 = fq_ref[0, 0, qi]
        for h in range(2):
            fk = fk_ref[0, 0, pl.ds(pl.multiple_of(kj * tk, tk), tk), h:h + 1]
            fq = fq2[h:h + 1, :]
            z = s_ref[h] - fk
            if diagonal:
                z = z + mb_ref[...]
            mz = jnp.max(z, axis=0, keepdims=True) + fq
            m_prev = st_ref[qi, _ST_M + h:_ST_M + h + 1, :]
            m_new = jnp.maximum(m_prev, mz)
            alpha = jnp.exp2(m_prev - m_new)
            p = jnp.exp2(z - (m_new - fq))
            l_new = (alpha * st_ref[qi, _ST_L + h:_ST_L + h + 1, :]
                     + jnp.sum(p, axis=0, keepdims=True))
            p_ref[h] = p.astype(BF16)
            st_ref[qi, _ST_M + h:_ST_M + h + 1, :] = m_new
            st_ref[qi, _ST_L + h:_ST_L + h + 1, :] = l_new
            ra = _PB_ALPHA + 2 * buf + h
            pb_ref[ra:ra + 1, :] = alpha
            rl = _PB_LFIN + 2 * buf + h
            pb_ref[rl:rl + 1, :] = l_new

    def stage_c(qi, kj, p_ref, pb_ref, buf, diagonal):
        vt = vt_ref[0, kj]
        for h in range(2):
            rows = slice(h * dh, (h + 1) * dh)
            pv = jnp.dot(vt[rows, :], p_ref[h], preferred_element_type=F32)
            ra = _PB_ALPHA + 2 * buf + h
            acc = pb_ref[ra:ra + 1, :] * acc_ref[qi, rows, :] + pv
            acc_ref[qi, rows, :] = acc
            if diagonal:
                rl = _PB_LFIN + 2 * buf + h
                o_ref[0, qi, rows, :] = acc / pb_ref[rl:rl + 1, :]

    def run(n_items, item, diagonal, pa_ref, pz_ref, pb_ref):
        stage_a(*item(0), s0_ref)

        def pair(jj, carry):
            j0 = 2 * jj
            stage_a(*item(j0 + 1), s1_ref)
            stage_b(*item(j0), s0_ref, pa_ref, pb_ref, 0, diagonal)
            stage_c(*item(jnp.maximum(j0 - 1, 0)), pz_ref, pb_ref, 1, diagonal)
            stage_a(*item(jnp.minimum(j0 + 2, n_items - 1)), s0_ref)
            stage_b(*item(j0 + 1), s1_ref, pz_ref, pb_ref, 1, diagonal)
            stage_c(*item(j0), pa_ref, pb_ref, 0, diagonal)
            return carry

        lax.fori_loop(0, n_items // 2, pair, 0)
        stage_c(*item(n_items - 1), pz_ref, pb_ref, 1, diagonal)

    p1_ref[...] = jnp.zeros_like(p1_ref)
    p3_ref[...] = jnp.zeros_like(p3_ref)
    pbo_ref[...] = jnp.ones_like(pbo_ref)
    pbd_ref[...] = jnp.ones_like(pbd_ref)
    run(n_off, lambda j: (qi_off_ref[j], kj_off_ref[j]), False, p0_ref, p1_ref, pbo_ref)
    run(n_diag, lambda j: (qi_diag_ref[j], qi_diag_ref[j]), True, p2_ref, p3_ref, pbd_ref)


def _fox(qt, k, vt, f_row, f_col):
    bsz, nq, width, tq = qt.shape
    seq = k.shape[1]
    dh = width // FOX_HEADS
    pairs = FOX_HEADS // 2
    qi_off = np.concatenate([np.full(i, i, np.int32) for i in range(1, nq)])
    kj_off = np.concatenate([np.arange(i, dtype=np.int32) for i in range(1, nq)])
    qi_diag = np.arange(nq, dtype=np.int32)
    assert len(qi_off) % 2 == 0 and nq % 2 == 0, "the pipelined loops take two items per trip"
    body = functools.partial(_fox_body, n_off=len(qi_off), n_diag=nq, tq=tq, tk=tq, dh=dh)
    blk_t = pl.BlockSpec((1, nq, 2 * dh, tq), lambda b, p, *_: (b, 0, p, 0))
    grid_spec = pltpu.PrefetchScalarGridSpec(
        num_scalar_prefetch=3,
        grid=(bsz, pairs),
        in_specs=[
            blk_t,
            pl.BlockSpec((1, seq, 2 * dh), lambda b, p, *_: (b, 0, p)),
            blk_t,
            pl.BlockSpec((1, 1, nq, 2, tq), lambda b, p, *_: (b, p, 0, 0, 0)),
            pl.BlockSpec((1, 1, seq, 2), lambda b, p, *_: (b, p, 0, 0)),
        ],
        out_specs=blk_t,
        scratch_shapes=[pltpu.VMEM((2, tq, tq), F32),
                        pltpu.VMEM((2, tq, tq), F32),
                        pltpu.VMEM((2, tq, tq), BF16),
                        pltpu.VMEM((2, tq, tq), BF16),
                        pltpu.VMEM((2, tq, tq), BF16),
                        pltpu.VMEM((2, tq, tq), BF16),
                        pltpu.VMEM((nq, SUBLANES, tq), F32),
                        pltpu.VMEM((SUBLANES, tq), F32),
                        pltpu.VMEM((SUBLANES, tq), F32),
                        pltpu.VMEM((nq, 2 * dh, tq), F32),
                        pltpu.VMEM((tq, tq), F32)],
    )
    return pl.pallas_call(
        body,
        grid_spec=grid_spec,
        out_shape=jax.ShapeDtypeStruct((bsz, nq, width, tq), F32),
        compiler_params=pltpu.CompilerParams(
            dimension_semantics=("parallel", "parallel"),
            vmem_limit_bytes=VMEM_LIMIT),
        name="fox",
    )(jnp.asarray(qi_off), jnp.asarray(kj_off), jnp.asarray(qi_diag), qt, k, vt, f_row, f_col)


def _mid_body(x_ref, yl_ref, yf_ref, gnf_ref, wo_ref, g2_ref, wqt_ref, sk_ref,
              x1_ref, h2t_ref, s1_ref, s2_ref, *, half):
    yfn = _rms(yf_ref[0, 0].T, gnf_ref[...]).astype(BF16)
    mix = jnp.dot(yl_ref[...], wo_ref[0:half, :], preferred_element_type=F32)
    mix = mix + jnp.dot(yfn, wo_ref[half:2 * half, :], preferred_element_type=F32)
    x1 = x_ref[...] + mix
    x1_ref[...] = x1
    h2t = _rms(x1, g2_ref[...]).T.astype(BF16)
    h2t_ref[...] = h2t
    qpt = jnp.dot(wqt_ref[...], h2t, preferred_element_type=F32)
    for hp in range(2 * PEER_HEADS):
        qs = qpt[hp * LANES:(hp + 1) * LANES, :].astype(BF16)
        sc = jnp.dot(sk_ref[hp], qs, preferred_element_type=F32)
        if hp % 2 == 0:
            s1_ref[hp // 2] = sc
        else:
            s2_ref[hp // 2] = sc


def _mid(x2d, y_lru, y_fox_t, gn_fox, w_out, g2, wq_t, subkeys, *, tm):
    t, d = x2d.shape
    half = y_lru.shape[1]
    _, n_blk, _, t_blk = y_fox_t.shape
    per_blk = t_blk // tm
    body = functools.partial(_mid_body, half=half)
    tok = lambda i: (i, 0)
    const2 = lambda i: (0, 0)
    sc_shape = jax.ShapeDtypeStruct((PEER_HEADS, PEER_NKEYS, t), F32)
    sc_spec = pl.BlockSpec((PEER_HEADS, PEER_NKEYS, tm), lambda i: (0, 0, i))
    return pl.pallas_call(
        body,
        grid=(t // tm,),
        in_specs=[
            pl.BlockSpec((tm, d), tok),
            pl.BlockSpec((tm, half), tok),
            pl.BlockSpec((1, 1, half, tm),
                         lambda i: (i // (per_blk * n_blk), (i // per_blk) % n_blk, 0,
                                    i % per_blk)),
            pl.BlockSpec((1, half), const2),
            pl.BlockSpec(w_out.shape, const2),
            pl.BlockSpec((1, d), const2),
            pl.BlockSpec(wq_t.shape, const2),
            pl.BlockSpec(subkeys.shape, lambda i: (0, 0, 0)),
        ],
        out_specs=[pl.BlockSpec((tm, d), tok),
                   pl.BlockSpec((d, tm), lambda i: (0, i)),
                   sc_spec, sc_spec],
        out_shape=[jax.ShapeDtypeStruct((t, d), F32),
                   jax.ShapeDtypeStruct((d, t), BF16),
                   sc_shape, sc_shape],
        compiler_params=pltpu.CompilerParams(
            dimension_semantics=("parallel",),
            vmem_limit_bytes=VMEM_LIMIT),
        name="mid",
    )(x2d, y_lru, y_fox_t, gn_fox, w_out, g2, wq_t, subkeys)


_CAND = [(a, b) for a in range(PEER_TOPK) for b in range(PEER_TOPK)
         if (a + 1) * (b + 1) <= PEER_TOPK]
_CAND_ROWS = -(-len(_CAND) // SUBLANES) * SUBLANES


def _top_values(x, n, want_rank=False):
    vals = []
    rank = jnp.full(x.shape, float(n), F32) if want_rank else None
    for r in range(n):
        mx = jnp.max(x, axis=0, keepdims=True)
        vals.append(mx)
        hit = x == mx
        x = jnp.where(hit, NEG_INF, x)
        if want_rank:
            rank = jnp.where(hit, float(r), rank)
    return vals, rank


def _gates_body(s1_ref, s2_ref, n_ref, p1_ref, rank_ref, p2_ref, cand_ref):
    s1 = s1_ref[0]
    s2 = s2_ref[0]
    a, _ = _top_values(s1, PEER_TOPK)
    b, rank = _top_values(s2, PEER_TOPK, want_rank=True)
    cand_ref[...] = jnp.full_like(cand_ref, NEG_INF)
    sums = [a[ia] + b[ib] for ia, ib in _CAND]
    for r, v in enumerate(sums):
        cand_ref[r:r + 1, :] = v
    c, _ = _top_values(cand_ref[...], PEER_TOPK)
    z = jnp.zeros_like(c[0])
    for r in range(PEER_TOPK):
        z = z + jnp.exp(c[r] - c[0])
    tau = c[PEER_TOPK - 1]
    picked = [jnp.zeros_like(tau) for _ in range(PEER_TOPK)]
    for (ia, _), v in zip(_CAND, sums):
        picked[ia] = picked[ia] + jnp.where(v >= tau, 1.0, 0.0)
    n = jnp.zeros_like(s1)
    for ia in range(PEER_TOPK):
        n = jnp.where(s1 == a[ia], picked[ia], n)
    n_ref[0] = n
    p1_ref[0] = 0.5 * jnp.exp(s1 - a[0]) / z
    rank_ref[0] = rank.astype(BF16)
    p2_ref[0] = jnp.exp(s2 - b[0]).astype(BF16)


def _gates(s1, s2, *, tm):
    heads, nk, t = s1.shape
    spec = pl.BlockSpec((1, nk, tm), lambda i, h: (h, 0, i))
    shape = jax.ShapeDtypeStruct(s1.shape, F32)
    shape_b = jax.ShapeDtypeStruct(s1.shape, BF16)
    return pl.pallas_call(
        _gates_body,
        grid=(t // tm, heads),
        in_specs=[spec, spec],
        out_specs=[spec, spec, spec, spec],
        out_shape=[shape, shape, shape_b, shape_b],
        scratch_shapes=[pltpu.VMEM((_CAND_ROWS, tm), F32)],
        compiler_params=pltpu.CompilerParams(
            dimension_semantics=("parallel", "parallel"),
            vmem_limit_bytes=VMEM_LIMIT),
        name="gates",
    )(s1, s2)


def _peer_body(h2_ref, u_ref, vt_ref, rank_ref, p2_ref, n_ref, p1_ref, x1_ref, gf_ref,
               o_ref, pre_ref, wact_ref, acc_ref, *, per_tile, rows_per_item):
    s = pl.program_id(0)
    n = s - 2

    @pl.when(s == 0)
    def _():
        pre_ref[...] = jnp.zeros_like(pre_ref)
        wact_ref[...] = jnp.zeros_like(wact_ref)
        acc_ref[...] = jnp.zeros_like(acc_ref)

    pre_next = jnp.dot(u_ref[...], h2_ref[...], preferred_element_type=F32)

    pv = jnp.dot(vt_ref[...], wact_ref[(s + 1) % 2], preferred_element_type=F32)
    first = (n % per_tile) == 0
    acc_ref[...] = jnp.where(first, pv, acc_ref[...] + pv)

    tm = pre_ref.shape[1]

    def packed_row(ref, h, r):
        tile = jnp.broadcast_to(ref[h, r:r + 1, :], (BF16_ROWS, tm)).astype(BF16)
        return tile[None]

    groups = PEER_NKEYS // BF16_ROWS
    for r in range(rows_per_item):
        w = jnp.zeros((groups, BF16_ROWS, tm), BF16)
        for h in range(PEER_HEADS):
            rank = rank_ref[h].reshape(groups, BF16_ROWS, tm)
            p2 = p2_ref[h].reshape(groups, BF16_ROWS, tm)
            sel = jnp.where(rank < packed_row(n_ref, h, r), p2, jnp.zeros_like(p2))
            w = w + sel * packed_row(p1_ref, h, r)
        rows = slice(r * PEER_NKEYS, (r + 1) * PEER_NKEYS)
        xb = pre_ref[rows, :].astype(BF16)
        tb = _gelu_tanh_factor(xb)
        wact_ref[s % 2, rows, :] = (xb + xb * tb) * w.reshape(PEER_NKEYS, tm)
    pre_ref[...] = pre_next

    @pl.when((n >= 0) & (n % per_tile == per_tile - 1))
    def _():
        xo = x1_ref[...] + acc_ref[...].T
        o_ref[...] = _rms(xo, gf_ref[...])


def _peer(h2t, u, vt, rank, p2, n_sel, p1, x1, gf, *, tm, te):
    d, t = h2t.shape
    n_exp = u.shape[0]
    per_tile = n_exp // te
    rows_per_item = te // PEER_NKEYS
    n_items = (t // tm) * per_tile
    body = functools.partial(_peer_body, per_tile=per_tile, rows_per_item=rows_per_item)

    def item(s, lag):
        return jnp.clip(s - lag, 0, n_items - 1)

    tile = lambda s, lag: item(s, lag) // per_tile
    blk = lambda s, lag: item(s, lag) % per_tile
    full_spec = pl.BlockSpec((PEER_HEADS, PEER_NKEYS, tm), lambda s: (0, 0, tile(s, 1)))
    rows_spec = pl.BlockSpec((PEER_HEADS, rows_per_item, tm),
                             lambda s: (0, blk(s, 1), tile(s, 1)))
    tok_spec = pl.BlockSpec((tm, d), lambda s: (tile(s, 2), 0))
    return pl.pallas_call(
        body,
        grid=(n_items + 2,),
        in_specs=[
            pl.BlockSpec((d, tm), lambda s: (0, tile(s, 0))),
            pl.BlockSpec((te, d), lambda s: (blk(s, 0), 0)),
            pl.BlockSpec((d, te), lambda s: (0, blk(s, 2))),
            full_spec, full_spec, rows_spec, rows_spec, tok_spec,
            pl.BlockSpec((1, d), lambda s: (0, 0)),
        ],
        out_specs=tok_spec,
        out_shape=jax.ShapeDtypeStruct((t, d), F32),
        scratch_shapes=[pltpu.VMEM((te, tm), F32), pltpu.VMEM((2, te, tm), BF16),
                        pltpu.VMEM((d, tm), F32)],
        compiler_params=pltpu.CompilerParams(
            dimension_semantics=("arbitrary",),
            vmem_limit_bytes=VMEM_LIMIT),
        name="peer",
    )(h2t, u, vt, rank, p2, n_sel, p1, x1, gf)


def _block_diag(w):
    heads, dh, _ = w.shape
    eye = jnp.eye(heads, dtype=w.dtype)
    return jnp.einsum("hij,hg->higj", w, eye).reshape(heads * dh, heads * dh)


def _layer(x, norm1_g, w_in, conv_w, conv_b, lru_wa, lru_ba, lru_wx, lru_bx, lru_lambda,
           fox_bf, gn_lru_g, gn_fox_g, w_out, norm2_g, peer_wq, peer_subkeys, peer_u,
           peer_v, out_g):
    bsz, seq, d = x.shape
    width = lru_lambda.shape[0]
    row = lambda v: v.reshape(1, -1).astype(F32)

    w_main = w_in[:, :5 * width].astype(BF16)
    w_fl = jnp.pad(w_in[:, 5 * width:], ((0, 0), (0, LANES - FOX_HEADS))).astype(BF16)
    b_fl = jnp.pad(fox_bf.astype(F32), (0, LANES - FOX_HEADS)).reshape(1, LANES)
    t_blk = FOX_BLOCK
    xr, gr, qt, k, vt, f_t = _inproj(x, row(norm1_g), w_main, w_fl, b_fl, tm=t_blk)

    w_gate = jnp.concatenate([_block_diag(lru_wa), _block_diag(lru_wx)], axis=1).astype(BF16)
    b_gate = jnp.concatenate([lru_ba, lru_bx]).reshape(1, -1).astype(F32)
    y_lru = _rglru(xr, gr, conv_w.astype(F32), row(conv_b), w_gate, b_gate,
                   row(lru_lambda), row(gn_lru_g), ts=256)

    pairs = FOX_HEADS // 2
    f_pair = f_t.reshape(bsz, pairs, 2, seq)
    f_col = f_pair.transpose(0, 1, 3, 2)
    f_row = f_pair.reshape(bsz, pairs, 2, seq // t_blk, t_blk).transpose(0, 1, 3, 2, 4)
    y_fox_t = _fox(qt, k, vt, f_row, f_col)

    t = bsz * seq
    n_hp = 2 * PEER_HEADS
    subkeys = peer_subkeys.reshape(n_hp, PEER_NKEYS, -1).astype(BF16)
    x1, h2t, s1, s2 = _mid(x.reshape(t, d), y_lru.reshape(t, width), y_fox_t,
                           row(gn_fox_g), w_out.astype(BF16), row(norm2_g),
                           peer_wq.T.astype(BF16), subkeys, tm=512)

    n_sel, p1, rank2, p2 = _gates(s1, s2, tm=256)
    out = _peer(h2t, peer_u.astype(BF16), peer_v.T.astype(BF16), rank2, p2, n_sel, p1, x1,
                row(out_g), tm=512, te=2048)
    return out.reshape(bsz, seq, d)


def kernel(x, norm1_g, w_in, conv_w, conv_b, lru_wa, lru_ba, lru_wx, lru_bx, lru_lambda,
           fox_bf, gn_lru_g, gn_fox_g, w_out, norm2_g, peer_wq, peer_subkeys, peer_u,
           peer_v, final_g):
    depth = w_in.shape[0]
    assert depth == 1, "the fused final RMSNorm assumes a single trunk layer"
    return _layer(x, norm1_g[0], w_in[0], conv_w[0], conv_b[0], lru_wa[0], lru_ba[0],
                  lru_wx[0], lru_bx[0], lru_lambda[0], fox_bf[0], gn_lru_g[0], gn_fox_g[0],
                  w_out[0], norm2_g[0], peer_wq[0], peer_subkeys[0], peer_u[0], peer_v[0],
                  final_g)
```
